```python
import math, functools
import jax, jax.numpy as jnp
from jax import lax
import numpy as np

D_MODEL = 1024
BATCH = 2
SEQ = 16384
DEPTH = 1
DEC_BATCH = 128
DEC_SEQ = 1
PAST_LEN = 8192
PAGE_SIZE = 128

HEAD_DIM = 64
H_FOX = 8
H_MOBA = 8
FOX_W = H_FOX * HEAD_DIM
MOBA_W = H_MOBA * HEAD_DIM
MOBA_BLOCK = 256
MOBA_TOPK = 3
Q_BLOCK = 128
D_FF = 2816
CONV_W = 3
N_MOD = 6
EPS = 1e-6
FORGET_BIAS_INIT = 1.0
IN_W = 3 * FOX_W + H_FOX + 3 * MOBA_W + 2 * D_MODEL

kernel_name = "fox_moba_convffn_hybrid_step"

F32 = jnp.float32


def rms_norm(x, g):
    x32 = x.astype(F32)
    y = x32 * lax.rsqrt(jnp.mean(x32 * x32, axis=-1, keepdims=True) + EPS)
    return (y * g.astype(F32)).astype(x.dtype)


def alibi_slopes(n):
    return jnp.asarray([2.0 ** (-8.0 * (h + 1) / n) for h in range(n)], F32)


def split_in_proj(p):
    sizes = (FOX_W,) * 3 + (H_FOX,) + (MOBA_W,) * 3 + (D_MODEL,) * 2
    idx = [int(i) for i in np.cumsum(sizes)[:-1]]
    return jnp.split(p, idx, axis=-1)


def gather_pages(cache, page_table, layer):
    g = cache[layer, page_table]
    return g.reshape((g.shape[0], g.shape[1] * g.shape[2]) + g.shape[3:])


def extend(new, cache, page_table, layer, block):
    parts = [] if cache is None else [gather_pages(cache, page_table, layer).astype(new.dtype)]
    parts.append(new)
    length = sum(p.shape[1] for p in parts)
    pad = (-length) % block
    if pad:
        parts.append(jnp.zeros((new.shape[0], pad) + new.shape[2:], new.dtype))
    return jnp.concatenate(parts, axis=1) if len(parts) > 1 else new


def sweep_queries(fn, q_pos, *q_arrays):
    n_q = q_pos.shape[0]
    if n_q <= Q_BLOCK or n_q % Q_BLOCK:
        return fn(q_pos, *q_arrays)
    n_blk = n_q // Q_BLOCK
    to_blocks = lambda a: jnp.moveaxis(a.reshape((a.shape[0], n_blk, Q_BLOCK) + a.shape[2:]), 1, 0)
    xs = (q_pos.reshape(n_blk, Q_BLOCK),) + tuple(to_blocks(a) for a in q_arrays)
    out = lax.map(lambda t: fn(*t), xs)
    out = jnp.moveaxis(out, 0, 1)
    return out.reshape((out.shape[0], n_q) + out.shape[3:])


def fox_attend(q_pos, q, cq, k, v, ck_t):
    L = k.shape[1]
    s = jnp.einsum('bqhd,blhd->bhql', q, k, preferred_element_type=F32) * (HEAD_DIM ** -0.5)
    s = s + jnp.swapaxes(cq, 1, 2).astype(F32)[..., None] - ck_t[:, :, None, :]
    causal = jnp.arange(L)[None, :] <= q_pos[:, None]
    s = jnp.where(causal, s, -jnp.inf)
    p = jax.nn.softmax(s, axis=-1).astype(v.dtype)
    return jnp.einsum('bhql,blhd->bqhd', p, v)


def moba_attend(q_pos, q, kblk, vblk, kmean, slopes):
    B, Q, H, _ = q.shape
    NB = kblk.shape[1]
    qb = q_pos // MOBA_BLOCK
    gate = jnp.einsum('bqhd,bnhd->bhqn', q.astype(F32), kmean)
    fully_past = jnp.arange(NB)[None, :] < qb[:, None]
    gate = jnp.where(fully_past, gate, -jnp.inf)
    n_sel = min(MOBA_TOPK, NB)
    _, sel = lax.top_k(gate, n_sel)
    sel_ok = sel < qb[:, None]
    own = jnp.broadcast_to(qb[:, None], (B, H, Q, 1)).astype(sel.dtype)
    blk = jnp.concatenate([sel, own], axis=-1)
    ok = jnp.concatenate([sel_ok, jnp.ones((B, H, Q, 1), bool)], axis=-1)
    bi = jnp.arange(B)[:, None, None, None]
    hi = jnp.arange(H)[None, :, None, None]
    kg = kblk[bi, blk, :, hi]
    vg = vblk[bi, blk, :, hi]
    kpos = blk[..., None] * MOBA_BLOCK + jnp.arange(MOBA_BLOCK)
    dist = (q_pos[:, None, None] - kpos).astype(F32)
    s = (jnp.einsum('bqhd,bhqkjd->bhqkj', q, kg, preferred_element_type=F32) * (HEAD_DIM ** -0.5)
         - slopes[:, None, None, None] * dist)
    s = jnp.where(ok[..., None] & (dist >= 0), s, -jnp.inf)
    p = jax.nn.softmax(s.reshape(B, H, Q, -1), axis=-1).astype(vg.dtype).reshape(kg.shape[:-1])
    return jnp.einsum('bhqkj,bhqkjd->bqhd', p, vg)


def trunk_layer(x, c, conv_prefix, past, w_ada, b_ada, g_mix_pre, g_mix_post, w_in, b_forget,
                w_branch_fox, w_branch_moba, w_out, g_ffn_pre, g_ffn_post, w_up, w_conv, b_conv, w_down):
    B, S, _ = x.shape
    if past is None:
        fk_c = fv_c = fl_c = mk_c = mv_c = pt = None
        layer = 0
        P = 0
    else:
        fk_c, fv_c, fl_c, mk_c, mv_c, pt, layer = past
        P = pt.shape[1] * fk_c.shape[2]
    q_pos = P + jnp.arange(S, dtype=jnp.int32)

    mod = (jax.nn.silu(c) @ w_ada + b_ada).reshape(B, N_MOD, 1, D_MODEL)
    shift_m, scale_m, gate_m, shift_f, scale_f, gate_f = [mod[:, i] for i in range(N_MOD)]

    h = rms_norm(x, g_mix_pre) * (1 + scale_m) + shift_m
    q_f, k_f, v_f, f_logit, q_m, k_m, v_m, gl_f, gl_m = split_in_proj(h @ w_in)
    heads = lambda t, n: t.reshape(B, S, n, HEAD_DIM)
    q_f, k_f, v_f = heads(q_f, H_FOX), heads(k_f, H_FOX), heads(v_f, H_FOX)
    q_m, k_m, v_m = heads(q_m, H_MOBA), heads(k_m, H_MOBA), heads(v_m, H_MOBA)
    log_f = jax.nn.log_sigmoid(f_logit.astype(F32) + b_forget.astype(F32))

    k_all = extend(k_f, fk_c, pt, layer, 1)
    v_all = extend(v_f, fv_c, pt, layer, 1)
    cum = jnp.cumsum(extend(log_f, fl_c, pt, layer, 1).astype(F32), axis=1)
    cum_t = jnp.swapaxes(cum, 1, 2)
    o_f = sweep_queries(lambda pos, q, cq: fox_attend(pos, q, cq, k_all, v_all, cum_t), q_pos, q_f, cum[:, P:])

    kblk = extend(k_m, mk_c, pt, layer, MOBA_BLOCK).reshape(B, -1, MOBA_BLOCK, H_MOBA, HEAD_DIM)
    vblk = extend(v_m, mv_c, pt, layer, MOBA_BLOCK).reshape(B, -1, MOBA_BLOCK, H_MOBA, HEAD_DIM)
    kmean = jnp.mean(kblk, axis=2, dtype=F32)
    slopes = alibi_slopes(H_MOBA)
    o_m = sweep_queries(lambda pos, q: moba_attend(pos, q, kblk, vblk, kmean, slopes), q_pos, q_m)

    mixed = (jax.nn.sigmoid(gl_f) * (o_f.reshape(B, S, FOX_W) @ w_branch_fox)
             + jax.nn.sigmoid(gl_m) * (o_m.reshape(B, S, MOBA_W) @ w_branch_moba))
    x = x + gate_m * rms_norm(mixed @ w_out, g_mix_post)

    h = rms_norm(x, g_ffn_pre) * (1 + scale_f) + shift_f
    u, v = jnp.split(h @ w_up, 2, axis=-1)
    u_ext = jnp.concatenate([conv_prefix.astype(u.dtype), u], axis=1)
    u_conv = b_conv + sum(w_conv[i] * u_ext[:, i:i + S] for i in range(CONV_W))
    ff = (jax.nn.gelu(u_conv, approximate=True) * v) @ w_down
    x = x + gate_f * rms_norm(ff, g_ffn_post)

    new_state = (k_f, v_f, log_f, k_m, v_m, u_ext[:, S:])
    return x, new_state


def setup_inputs(seed: int = 0) -> dict:
    key = jax.random.key(seed)
    ks = iter(jax.random.split(key, 32))
    nrm = lambda shape, scale=1.0: scale * jax.random.normal(next(ks), shape, F32)
    n_pages = PAST_LEN // PAGE_SIZE
    n_phys = -(-5 * DEC_BATCH * n_pages // 4)
    page_table = jax.random.permutation(next(ks), n_phys)[: DEC_BATCH * n_pages]
    page_table = page_table.reshape(DEC_BATCH, n_pages).astype(jnp.int32)
    pool = (DEPTH, n_phys, PAGE_SIZE)
    return {
        "x_prompt": nrm((BATCH, SEQ, D_MODEL)),
        "x_sample": nrm((DEC_BATCH, DEC_SEQ, D_MODEL)),
        "cache_fox_k": nrm(pool + (H_FOX, HEAD_DIM)),
        "cache_fox_v": nrm(pool + (H_FOX, HEAD_DIM)),
        "cache_fox_logf": jax.nn.log_sigmoid(FORGET_BIAS_INIT + nrm(pool + (H_FOX,))),
        "cache_moba_k": nrm(pool + (H_MOBA, HEAD_DIM)),
        "cache_moba_v": nrm(pool + (H_MOBA, HEAD_DIM)),
        "state_conv": nrm((DEPTH, DEC_BATCH, CONV_W - 1, D_FF)),
        "page_table": page_table,
        "c_prompt": nrm((BATCH, D_MODEL)),
        "c_sample": nrm((DEC_BATCH, D_MODEL)),
        "w_ada": nrm((DEPTH, D_MODEL, N_MOD * D_MODEL), D_MODEL ** -0.5),
        "b_ada": nrm((DEPTH, N_MOD * D_MODEL), 0.01),
        "g_mix_pre": 1.0 + nrm((DEPTH, D_MODEL), 0.1),
        "g_mix_post": 1.0 + nrm((DEPTH, D_MODEL), 0.1),
        "w_in": nrm((DEPTH, D_MODEL, IN_W), D_MODEL ** -0.5),
        "b_forget": FORGET_BIAS_INIT + nrm((DEPTH, H_FOX), 0.1),
        "w_branch_fox": nrm((DEPTH, FOX_W, D_MODEL), FOX_W ** -0.5),
        "w_branch_moba": nrm((DEPTH, MOBA_W, D_MODEL), MOBA_W ** -0.5),
        "w_out": nrm((DEPTH, D_MODEL, D_MODEL), D_MODEL ** -0.5),
        "g_ffn_pre": 1.0 + nrm((DEPTH, D_MODEL), 0.1),
        "g_ffn_post": 1.0 + nrm((DEPTH, D_MODEL), 0.1),
        "w_up": nrm((DEPTH, D_MODEL, 2 * D_FF), D_MODEL ** -0.5),
        "w_conv": nrm((DEPTH, CONV_W, D_FF), CONV_W ** -0.5),
        "b_conv": nrm((DEPTH, D_FF), 0.01),
        "w_down": nrm((DEPTH, D_FF, D_MODEL), D_FF ** -0.5),
    }


def reference(x_prompt, x_sample, cache_fox_k, cache_fox_v, cache_fox_logf, cache_moba_k, cache_moba_v,
              state_conv, page_table, c_prompt, c_sample, w_ada, b_ada, g_mix_pre, g_mix_post, w_in, b_forget,
              w_branch_fox, w_branch_moba, w_out, g_ffn_pre, g_ffn_post, w_up, w_conv, b_conv, w_down):
    y_prompt, y_sample = x_prompt, x_sample
    prompt_rows, sample_rows = [], []
    for l in range(DEPTH):
        layer_w = (w_ada[l], b_ada[l], g_mix_pre[l], g_mix_post[l], w_in[l], b_forget[l], w_branch_fox[l],
                   w_branch_moba[l], w_out[l], g_ffn_pre[l], g_ffn_post[l], w_up[l], w_conv[l], b_conv[l], w_down[l])
        conv0 = jnp.zeros((x_prompt.shape[0], CONV_W - 1, D_FF), x_prompt.dtype)
        y_prompt, st_p = trunk_layer(y_prompt, c_prompt, conv0, None, *layer_w)
        prompt_rows.append(st_p)
        past = (cache_fox_k, cache_fox_v, cache_fox_logf, cache_moba_k, cache_moba_v, page_table, l)
        y_sample, st_s = trunk_layer(y_sample, c_sample, state_conv[l], past, *layer_w)
        sample_rows.append(st_s)
    stack = lambda rows, i: jnp.stack([r[i] for r in rows])
    return (y_prompt, y_sample,
            stack(prompt_rows, 0), stack(prompt_rows, 1), stack(prompt_rows, 2),
            stack(prompt_rows, 3), stack(prompt_rows, 4), stack(prompt_rows, 5),
            stack(sample_rows, 0), stack(sample_rows, 1), stack(sample_rows, 2),
            stack(sample_rows, 3), stack(sample_rows, 4), stack(sample_rows, 5))
```

```python
import functools

import jax
import jax.numpy as jnp
from jax import lax
from jax.experimental import pallas as pl
from jax.experimental.pallas import tpu as pltpu

F32 = jnp.float32
BF16 = jnp.bfloat16

HEAD_DIM = 64
MOBA_BLOCK = 256
MOBA_TOPK = 3
CONV_W = 3
N_MOD = 6
EPS = 1e-6
LANES = 128
SUBLANES = 8
NEG = -1e30
VMEM_LIMIT = 56 * 1024 * 1024

_NT = (((1,), (1,)), ((), ()))


def _params(*sem):
    return pltpu.CompilerParams(dimension_semantics=sem, vmem_limit_bytes=VMEM_LIMIT)


def _const_spec(shape):
    nd = len(shape)
    return pl.BlockSpec(shape, lambda *_: (0,) * nd, pipeline_mode=pl.Buffered(1))


def _rms(x, g):
    return x * lax.rsqrt(jnp.mean(x * x, axis=-1, keepdims=True) + EPS) * g


def _dot(a, b):
    return jnp.dot(a, b, preferred_element_type=F32)


def _dot_nt(a, b):
    return lax.dot_general(a, b, _NT, preferred_element_type=F32)


def _ada_kernel(c_ref, w_ref, b_ref, o_ref):
    c = c_ref[...]
    s = (c * jax.nn.sigmoid(c)).astype(BF16)
    o_ref[...] = _dot(s, w_ref[...].astype(BF16)) + b_ref[...]


def _ada(c, w, b):
    rows, d = c.shape
    n = w.shape[1]
    tn = 1536 if n % 1536 == 0 else n
    return pl.pallas_call(
        _ada_kernel,
        grid=(n // tn,),
        in_specs=[pl.BlockSpec((rows, d), lambda j: (0, 0)),
                  pl.BlockSpec((d, tn), lambda j: (0, j)),
                  pl.BlockSpec((1, tn), lambda j: (0, j))],
        out_specs=pl.BlockSpec((rows, tn), lambda j: (0, j)),
        out_shape=jax.ShapeDtypeStruct((rows, n), F32),
        compiler_params=_params("arbitrary"),
        name="ada_mod",
    )(c, w, b)


def _inproj_kernel(x_ref, shift_ref, scale_ref, g_ref, wf_ref, wl_ref, wm_ref, wg_ref, bl_ref,
                   qf_ref, kf_ref, vf_ref, kfb_ref, vfb_ref, lf_ref, lft_ref,
                   qm_ref, km_ref, vm_ref, kmb_ref, vmb_ref, sgf_ref, sgm_ref, *, fw, mw, d, nh):
    x = x_ref[...]
    h = (_rms(x, g_ref[...]) * (1.0 + scale_ref[...]) + shift_ref[...]).astype(BF16)
    qs = HEAD_DIM ** -0.5

    a = _dot(h, wf_ref[...])
    qf_ref[...] = (a[:, :fw] * qs).astype(BF16)
    kf = a[:, fw:2 * fw]
    vf = a[:, 2 * fw:]
    kf_ref[...] = kf
    vf_ref[...] = vf
    kfb_ref[...] = kf.astype(BF16)
    vfb_ref[...] = vf.astype(BF16)

    z = _dot(h, wl_ref[...]) + bl_ref[...]
    lf = jnp.minimum(z, 0.0) - jnp.log1p(jnp.exp(-jnp.abs(z)))
    lf_ref[...] = lf[:, :nh]
    lft_ref[...] = lf.T[:nh, :]

    a = _dot(h, wm_ref[...])
    qm_ref[...] = (a[:, :mw] * qs).astype(BF16)
    km = a[:, mw:2 * mw]
    vm = a[:, 2 * mw:]
    km_ref[...] = km
    vm_ref[...] = vm
    kmb_ref[...] = km.astype(BF16)
    vmb_ref[...] = vm.astype(BF16)

    a = jax.nn.sigmoid(_dot(h, wg_ref[...]))
    sgf_ref[...] = a[:, :d].astype(BF16)
    sgm_ref[...] = a[:, d:].astype(BF16)


def _inproj(x, shift, scale, g, wf, wl, wm, wg, bl, *, tm, nh):
    b, s, d = x.shape
    fw = wf.shape[1] // 3
    mw = wm.shape[1] // 3
    mrows = shift.shape[1]
    if mrows == 1:
        mod_spec = pl.BlockSpec((None, 1, d), lambda bi, i: (bi, 0, 0))
    else:
        mod_spec = pl.BlockSpec((None, tm, d), lambda bi, i: (bi, i, 0))
    row = lambda w: pl.BlockSpec((None, tm, w), lambda bi, i: (bi, i, 0))
    outs = [(fw, BF16), (fw, F32), (fw, F32), (fw, BF16), (fw, BF16), (nh, F32), None,
            (mw, BF16), (mw, F32), (mw, F32), (mw, BF16), (mw, BF16), (d, BF16), (d, BF16)]
    out_shape, out_specs = [], []
    for o in outs:
        if o is None:
            out_shape.append(jax.ShapeDtypeStruct((b, nh, s), F32))
            out_specs.append(pl.BlockSpec((None, nh, tm), lambda bi, i: (bi, 0, i)))
        else:
            out_shape.append(jax.ShapeDtypeStruct((b, s, o[0]), o[1]))
            out_specs.append(row(o[0]))
    return pl.pallas_call(
        functools.partial(_inproj_kernel, fw=fw, mw=mw, d=d, nh=nh),
        grid=(b, s // tm),
        in_specs=[row(d), mod_spec, mod_spec, _const_spec((1, d)),
                  _const_spec(wf.shape), _const_spec(wl.shape), _const_spec(wm.shape),
                  _const_spec(wg.shape), _const_spec((1, LANES))],
        out_specs=out_specs,
        out_shape=out_shape,
        compiler_params=_params("arbitrary", "arbitrary"),
        name="in_proj",
    )(x, shift, scale, g, wf, wl, wm, wg, bl)


def _scan_lanes(x, reverse):
    n = x.shape[-1]
    lane = lax.broadcasted_iota(jnp.int32, x.shape, x.ndim - 1)
    sh = 1
    while sh < n:
        if reverse:
            x = x + jnp.where(lane < n - sh, pltpu.roll(x, n - sh, axis=x.ndim - 1), 0.0)
        else:
            x = x + jnp.where(lane >= sh, pltpu.roll(x, sh, axis=x.ndim - 1), 0.0)
        sh *= 2
    return x


def _cumsum_kernel(x_ref, o_ref):
    o_ref[...] = _scan_lanes(x_ref[...], reverse=False)


def _cumsum(x):
    b, nh, s = x.shape
    return pl.pallas_call(
        _cumsum_kernel,
        grid=(b,),
        in_specs=[pl.BlockSpec((None, nh, s), lambda bi: (bi, 0, 0))],
        out_specs=pl.BlockSpec((None, nh, s), lambda bi: (bi, 0, 0)),
        out_shape=jax.ShapeDtypeStruct((b, nh, s), F32),
        compiler_params=_params("arbitrary"),
        name="logf_cumsum",
    )(x)


def _softmax_step(s, v, m, l, acc):
    m_new = jnp.maximum(m, jnp.max(s, axis=-1, keepdims=True))
    p = jnp.exp(s - m_new)
    alpha = jnp.exp(m - m_new)
    l = alpha * l + jnp.sum(p, axis=-1, keepdims=True)
    acc = alpha * acc + _dot(p.astype(BF16), v)
    return m_new, l, acc


def _fox_kernel(q_ref, k_ref, v_ref, cq_ref, ck_ref, o_ref, *, t):
    i = pl.program_id(2)
    q2 = q_ref[...]
    lane = lax.broadcasted_iota(jnp.int32, (1, LANES), 1)
    row = lax.broadcasted_iota(jnp.int32, (t, t), 0)
    col = lax.broadcasted_iota(jnp.int32, (t, t), 1)
    outs = []
    for hh in range(2):
        in_head = (lane < HEAD_DIM) if hh == 0 else (lane >= HEAD_DIM)
        qh = jnp.where(in_head, q2, jnp.zeros_like(q2))
        cq = cq_ref[:, hh:hh + 1]

        def scores(j):
            ks = pl.multiple_of(j * t, t)
            s = _dot_nt(qh, k_ref[pl.ds(ks, t), :])
            s = s + (cq - ck_ref[hh:hh + 1, pl.ds(ks, t)])
            return s, v_ref[pl.ds(ks, t), :]

        def body(j, carry):
            s, v = scores(j)
            return _softmax_step(s, v, *carry)

        init = (jnp.full((t, 1), -jnp.inf, F32), jnp.zeros((t, 1), F32), jnp.zeros((t, LANES), F32))
        carry = lax.fori_loop(0, i, body, init)
        s, v = scores(i)
        s = jnp.where(col <= row, s, -jnp.inf)
        _, l, acc = _softmax_step(s, v, *carry)
        outs.append(acc / l)
    o_ref[...] = jnp.where(lane < HEAD_DIM, outs[0], outs[1]).astype(BF16)


def _fox_attention(q, k, v, cq, ck, *, t):
    b, s, w = q.shape
    npair = w // LANES
    return pl.pallas_call(
        functools.partial(_fox_kernel, t=t),
        grid=(b, npair, s // t),
        in_specs=[pl.BlockSpec((None, t, LANES), lambda bi, hp, i: (bi, i, hp)),
                  pl.BlockSpec((None, s, LANES), lambda bi, hp, i: (bi, 0, hp)),
                  pl.BlockSpec((None, s, LANES), lambda bi, hp, i: (bi, 0, hp)),
                  pl.BlockSpec((None, None, t, 2), lambda bi, hp, i: (bi, hp, i, 0)),
                  pl.BlockSpec((None, None, 2, s), lambda bi, hp, i: (bi, hp, 0, 0))],
        out_specs=pl.BlockSpec((None, t, LANES), lambda bi, hp, i: (bi, i, hp)),
        out_shape=jax.ShapeDtypeStruct((b, s, w), BF16),
        compiler_params=_params("arbitrary", "arbitrary", "arbitrary"),
        name="fox_attention",
    )(q, k, v, cq, ck)


def _kmean_kernel(k_ref, o_ref, *, nvalid):
    i = pl.program_id(1)

    @pl.when(i < nvalid)
    def _():
        k = k_ref[...]
        o_ref[...] = jnp.mean(k.reshape(SUBLANES, MOBA_BLOCK, k.shape[-1]), axis=1)

    @pl.when(i >= nvalid)
    def _():
        o_ref[...] = jnp.zeros_like(o_ref)


def _kmean(k, nbp):
    b, s, w = k.shape
    rows = SUBLANES * MOBA_BLOCK
    nvalid = s // rows
    return pl.pallas_call(
        functools.partial(_kmean_kernel, nvalid=nvalid),
        grid=(b, nbp // SUBLANES),
        in_specs=[pl.BlockSpec((None, rows, w), lambda bi, i: (bi, jnp.minimum(i, nvalid - 1), 0))],
        out_specs=pl.BlockSpec((None, SUBLANES, w), lambda bi, i: (bi, i, 0)),
        out_shape=jax.ShapeDtypeStruct((b, nbp, w), F32),
        compiler_params=_params("arbitrary", "arbitrary"),
        name="moba_kmean",
    )(k)


def _pick_topk(g, idx, nbp):
    sel = jnp.zeros(g.shape, jnp.bool_)
    for _ in range(MOBA_TOPK):
        mx = jnp.max(g, axis=-1, keepdims=True)
        is_max = (g == mx) & (mx > -jnp.inf)
        first = jnp.min(jnp.where(is_max, idx, nbp), axis=-1, keepdims=True)
        pick = idx == first
        sel = sel | pick
        g = jnp.where(pick, -jnp.inf, g)
    return sel


def _moba_kernel(slope_ref, q_ref, k_ref, v_ref, km_ref, o_ref, *, nbp):
    t = MOBA_BLOCK
    hp = pl.program_id(1)
    i = pl.program_id(2)
    q2 = q_ref[...]
    km = km_ref[...]
    km_hi = km.astype(BF16)
    km_lo = (km - km_hi.astype(F32)).astype(BF16)
    lane = lax.broadcasted_iota(jnp.int32, (1, LANES), 1)
    row = lax.broadcasted_iota(jnp.int32, (t, t), 0)
    col = lax.broadcasted_iota(jnp.int32, (t, t), 1)
    dmat = (col - row).astype(F32)
    blk = lax.broadcasted_iota(jnp.int32, (t, nbp), 1)
    onehot_row = lax.broadcasted_iota(jnp.int32, (nbp, t), 0)
    ks0 = pl.multiple_of(i * t, t)
    outs = []
    for hh in range(2):
        slope = slope_ref[2 * hp + hh]
        in_head = (lane < HEAD_DIM) if hh == 0 else (lane >= HEAD_DIM)
        qh = jnp.where(in_head, q2, jnp.zeros_like(q2))
        gate = _dot_nt(qh, km_hi) + _dot_nt(qh, km_lo)
        sel = _pick_topk(jnp.where(blk < i, gate, -jnp.inf), blk, nbp)
        selbias = jnp.where(sel, 0.0, NEG).astype(BF16)

        s = _dot_nt(qh, k_ref[pl.ds(ks0, t), :]) + slope * dmat
        s = jnp.where(col <= row, s, -jnp.inf)
        init = (jnp.full((t, 1), -jnp.inf, F32), jnp.zeros((t, 1), F32), jnp.zeros((t, LANES), F32))
        carry = _softmax_step(s, v_ref[pl.ds(ks0, t), :], *init)

        def body(j, carry):
            ks = pl.multiple_of(j * t, t)
            sb = _dot(selbias, (onehot_row == j).astype(BF16))
            back = ((i - j) * t).astype(F32)
            s = _dot_nt(qh, k_ref[pl.ds(ks, t), :]) + slope * (dmat - back) + sb
            return _softmax_step(s, v_ref[pl.ds(ks, t), :], *carry)

        _, l, acc = lax.fori_loop(0, i, body, carry)
        outs.append(acc / l)
    o_ref[...] = jnp.where(lane < HEAD_DIM, outs[0], outs[1]).astype(BF16)


def _moba_attention(slopes, q, k, v, kmean):
    b, s, w = q.shape
    npair = w // LANES
    nbp = kmean.shape[1]
    t = MOBA_BLOCK
    return pl.pallas_call(
        functools.partial(_moba_kernel, nbp=nbp),
        grid=(b, npair, s // t),
        in_specs=[pl.BlockSpec(memory_space=pltpu.SMEM),
                  pl.BlockSpec((None, t, LANES), lambda bi, hp, i: (bi, i, hp)),
                  pl.BlockSpec((None, s, LANES), lambda bi, hp, i: (bi, 0, hp)),
                  pl.BlockSpec((None, s, LANES), lambda bi, hp, i: (bi, 0, hp)),
                  pl.BlockSpec((None, nbp, LANES), lambda bi, hp, i: (bi, 0, hp))],
        out_specs=pl.BlockSpec((None, t, LANES), lambda bi, hp, i: (bi, i, hp)),
        out_shape=jax.ShapeDtypeStruct((b, s, w), BF16),
        compiler_params=_params("arbitrary", "arbitrary", "arbitrary"),
        name="moba_attention",
    )(slopes, q, k, v, kmean)


def _outproj_kernel(x_ref, of_ref, om_ref, sgf_ref, sgm_ref, gate_ref, g_ref, wbf_ref, wbm_ref, wo_ref, y_ref):
    yf = _dot(of_ref[...], wbf_ref[...])
    ym = _dot(om_ref[...], wbm_ref[...])
    mixed = sgf_ref[...].astype(F32) * yf + sgm_ref[...].astype(F32) * ym
    z = _dot(mixed.astype(BF16), wo_ref[...])
    y_ref[...] = x_ref[...] + gate_ref[...] * _rms(z, g_ref[...])


def _outproj(x, of, om, sgf, sgm, gate, g, wbf, wbm, wo, *, tm):
    b, s, d = x.shape
    if gate.shape[1] == 1:
        mod_spec = pl.BlockSpec((None, 1, d), lambda bi, i: (bi, 0, 0))
    else:
        mod_spec = pl.BlockSpec((None, tm, d), lambda bi, i: (bi, i, 0))
    row = lambda w: pl.BlockSpec((None, tm, w), lambda bi, i: (bi, i, 0))
    return pl.pallas_call(
        _outproj_kernel,
        grid=(b, s // tm),
        in_specs=[row(d), row(of.shape[-1]), row(om.shape[-1]), row(d), row(d), mod_spec,
                  _const_spec((1, d)), _const_spec(wbf.shape), _const_spec(wbm.shape), _const_spec(wo.shape)],
        out_specs=row(d),
        out_shape=jax.ShapeDtypeStruct((b, s, d), F32),
        compiler_params=_params("arbitrary", "arbitrary"),
        name="out_proj",
    )(x, of, om, sgf, sgm, gate, g, wbf, wbm, wo)


def _gelu_tanh(x):
    c = 0.7978845608028654
    return 0.5 * x * (1.0 + jnp.tanh(c * (x + 0.044715 * (x * x * x))))


def _ffn_kernel(*refs, dff, ch, tm, stepwise):
    if stepwise:
        (x_ref, shift_ref, scale_ref, gate_ref, gpre_ref, gpost_ref, wup_ref, wc_ref, bc_ref, wdn_ref,
         p2_ref, p1_ref, y_ref, u_ref, acc_ref) = refs
    else:
        (x_ref, shift_ref, scale_ref, gate_ref, gpre_ref, gpost_ref, wup_ref, wc_ref, bc_ref, wdn_ref,
         y_ref, tail_ref, acc_ref, ubuf_ref, carry_ref) = refs
        first = pl.program_id(1) == 0
    x = x_ref[...]
    h = (_rms(x, gpre_ref[...]) * (1.0 + scale_ref[...]) + shift_ref[...]).astype(BF16)
    acc_ref[...] = jnp.zeros_like(acc_ref)
    for c in range(dff // ch):
        cs = slice(c * ch, (c + 1) * ch)
        u = _dot(h, wup_ref[:, cs])
        v = _dot(h, wup_ref[:, dff + c * ch:dff + (c + 1) * ch])
        if stepwise:
            u2 = p2_ref[:, cs]
            u1 = p1_ref[:, cs]
            u_ref[:, cs] = u
        else:
            @pl.when(first)
            def _():
                carry_ref[c] = jnp.zeros((SUBLANES, ch), F32)

            ubuf_ref[0:SUBLANES, :] = carry_ref[c]
            ubuf_ref[SUBLANES:SUBLANES + tm, :] = u
            u1 = ubuf_ref[SUBLANES - 1:SUBLANES - 1 + tm, :]
            u2 = ubuf_ref[SUBLANES - 2:SUBLANES - 2 + tm, :]
            tail = u[tm - SUBLANES:, :]
            carry_ref[c] = tail
            tail_ref[:, cs] = tail
        uc = bc_ref[:, cs] + (wc_ref[0:1, cs] * u2 + wc_ref[1:2, cs] * u1 + wc_ref[2:3, cs] * u)
        g = (_gelu_tanh(uc) * v).astype(BF16)
        acc_ref[...] += _dot(g, wdn_ref[cs, :])
    y_ref[...] = x + gate_ref[...] * _rms(acc_ref[...], gpost_ref[...])


def _ffn(x, shift, scale, gate, gpre, gpost, wup, wc, bc, wdn, prev=None, *, tm):
    b, s, d = x.shape
    dff = wdn.shape[0]
    ch = 256 if dff % 256 == 0 else LANES
    stepwise = prev is not None
    if shift.shape[1] == 1:
        mod_spec = pl.BlockSpec((None, 1, d), lambda bi, i: (bi, 0, 0))
    else:
        mod_spec = pl.BlockSpec((None, tm, d), lambda bi, i: (bi, i, 0))
    row = lambda w: pl.BlockSpec((None, tm, w), lambda bi, i: (bi, i, 0))
    in_specs = [row(d), mod_spec, mod_spec, mod_spec, _const_spec((1, d)), _const_spec((1, d)),
                _const_spec(wup.shape), _const_spec(wc.shape), _const_spec(bc.shape), _const_spec(wdn.shape)]
    args = [x, shift, scale, gate, gpre, gpost, wup, wc, bc, wdn]
    scratch = [pltpu.VMEM((tm, d), F32)]
    if stepwise:
        in_specs += [row(dff), row(dff)]
        args += list(prev)
        out_specs = [row(d), row(dff)]
        out_shape = [jax.ShapeDtypeStruct((b, s, d), F32), jax.ShapeDtypeStruct((b, s, dff), F32)]
    else:
        out_specs = [row(d), pl.BlockSpec((None, SUBLANES, dff), lambda bi, i: (bi, 0, 0))]
        out_shape = [jax.ShapeDtypeStruct((b, s, d), F32), jax.ShapeDtypeStruct((b, SUBLANES, dff), F32)]
        scratch += [pltpu.VMEM((tm + SUBLANES, ch), F32), pltpu.VMEM((dff // ch, SUBLANES, ch), F32)]
    return pl.pallas_call(
        functools.partial(_ffn_kernel, dff=dff, ch=ch, tm=tm, stepwise=stepwise),
        grid=(b, s // tm),
        in_specs=in_specs,
        out_specs=out_specs,
        out_shape=out_shape,
        scratch_shapes=scratch,
        compiler_params=_params("arbitrary", "arbitrary"),
        name="ffn_step" if stepwise else "ffn_seq",
    )(*args)


def _page_spec(block, layer, npages_per_seq, per_step, p):
    def index(bi, g, pt_ref, *_):
        page = pt_ref[bi * npages_per_seq + g * per_step + p]
        return (layer, page) + (0,) * (len(block) - 2)
    return pl.BlockSpec(block, index)


def _gather_rows_kernel(pt_ref, *refs, per_step, rows):
    o_ref = refs[per_step]
    for p in range(per_step):
        o_ref[p * rows:(p + 1) * rows, :] = refs[p][...]


def _gather_logf(cache, layer, pt_flat, b, npages, per_step):
    _, _, rows, nh = cache.shape
    grid_spec = pltpu.PrefetchScalarGridSpec(
        num_scalar_prefetch=1,
        grid=(b, npages // per_step),
        in_specs=[_page_spec((None, None, rows, nh), layer, npages, per_step, p) for p in range(per_step)],
        out_specs=pl.BlockSpec((None, per_step * rows, nh), lambda bi, g, pt: (bi, g, 0)),
    )
    return pl.pallas_call(
        functools.partial(_gather_rows_kernel, per_step=per_step, rows=rows),
        grid_spec=grid_spec,
        out_shape=jax.ShapeDtypeStruct((b, npages * rows, nh), F32),
        compiler_params=_params("arbitrary", "arbitrary"),
        name="logf_gather",
    )(pt_flat, *([cache] * per_step))


def _suffix_kernel(x_ref, new_ref, o_ref):
    x = x_ref[...]
    n = x.shape[-1]
    lane = lax.broadcasted_iota(jnp.int32, x.shape, 1)
    shifted = jnp.where(lane < n - 1, pltpu.roll(x, n - 1, axis=1), 0.0)
    o_ref[...] = _scan_lanes(shifted, reverse=True) + new_ref[...]


def _suffix_bias(xt, new):
    b, nh, n = xt.shape
    return pl.pallas_call(
        _suffix_kernel,
        grid=(b,),
        in_specs=[pl.BlockSpec((None, nh, n), lambda bi: (bi, 0, 0)),
                  pl.BlockSpec((None, nh, 1), lambda bi: (bi, 0, 0))],
        out_specs=pl.BlockSpec((None, nh, n), lambda bi: (bi, 0, 0)),
        out_shape=jax.ShapeDtypeStruct((b, nh, n), F32),
        compiler_params=_params("arbitrary"),
        name="logf_suffix",
    )(xt, new)


def _page_scores(q, kpage):
    rows, nh, hd = kpage.shape
    kf = kpage.reshape(rows * nh, hd).astype(BF16)
    return _dot_nt(q, kf)[:nh, :]


def _fox_dec_kernel(pt_ref, q_ref, kn_ref, vn_ref, d_ref, *refs, per_step):
    kp = refs[:per_step]
    vp = refs[per_step:2 * per_step]
    o_ref, m_ref, l_ref, acc_ref = refs[2 * per_step:]
    g = pl.program_id(1)
    rows, nh, hd = kp[0].shape
    w = rows * nh

    @pl.when(g == 0)
    def _():
        m_ref[...] = jnp.full_like(m_ref, NEG)
        l_ref[...] = jnp.zeros_like(l_ref)
        acc_ref[...] = jnp.zeros_like(acc_ref)

    q = q_ref[...]
    own = (lax.broadcasted_iota(jnp.int32, (nh, w), 1) % nh) == lax.broadcasted_iota(jnp.int32, (nh, w), 0)
    ss = []
    m_new = m_ref[...]
    for p in range(per_step):
        s = _page_scores(q, kp[p][...]) + d_ref[:, p * w:(p + 1) * w]
        s = jnp.where(own, s, NEG)
        ss.append(s)
        m_new = jnp.maximum(m_new, jnp.max(s, axis=-1, keepdims=True))
    alpha = jnp.exp(m_ref[...] - m_new)
    l = alpha * l_ref[...]
    acc = alpha * acc_ref[...]
    pad = jnp.zeros((nh, w), F32)
    for p in range(per_step):
        pr = jnp.exp(ss[p] - m_new)
        l = l + jnp.sum(pr, axis=-1, keepdims=True)
        pb = jnp.concatenate([pr, pad], axis=0).astype(BF16)
        vf = vp[p][...].reshape(w, hd).astype(BF16)
        acc = acc + _dot(pb, vf)[:nh, :]
    m_ref[...] = m_new
    l_ref[...] = l
    acc_ref[...] = acc

    @pl.when(g == pl.num_programs(1) - 1)
    def _():
        qf = q[:nh, :].astype(F32)
        s_self = jnp.sum(qf * kn_ref[...].astype(BF16).astype(F32), axis=-1, keepdims=True)
        m_fin = jnp.maximum(m_new, s_self)
        a = jnp.exp(m_new - m_fin)
        p_self = jnp.exp(s_self - m_fin)
        l_fin = a * l + p_self
        acc_fin = a * acc + p_self.astype(BF16).astype(F32) * vn_ref[...].astype(BF16).astype(F32)
        o_ref[...] = acc_fin / l_fin


def _fox_decode(q16, kn, vn, dexp, cache_k, cache_v, layer, pt_flat, npages, per_step):
    b = q16.shape[0]
    _, _, rows, nh, hd = cache_k.shape
    w = rows * nh
    page = lambda p: _page_spec((None, None, rows, nh, hd), layer, npages, per_step, p)
    grid_spec = pltpu.PrefetchScalarGridSpec(
        num_scalar_prefetch=1,
        grid=(b, npages // per_step),
        in_specs=[pl.BlockSpec((None, 2 * SUBLANES, hd), lambda bi, g, pt: (bi, 0, 0)),
                  pl.BlockSpec((None, nh, hd), lambda bi, g, pt: (bi, 0, 0)),
                  pl.BlockSpec((None, nh, hd), lambda bi, g, pt: (bi, 0, 0)),
                  pl.BlockSpec((None, nh, per_step * w), lambda bi, g, pt: (bi, 0, g))]
                 + [page(p) for p in range(per_step)] + [page(p) for p in range(per_step)],
        out_specs=pl.BlockSpec((None, nh, hd), lambda bi, g, pt: (bi, 0, 0)),
        scratch_shapes=[pltpu.VMEM((nh, 1), F32), pltpu.VMEM((nh, 1), F32), pltpu.VMEM((nh, hd), F32)],
    )
    return pl.pallas_call(
        functools.partial(_fox_dec_kernel, per_step=per_step),
        grid_spec=grid_spec,
        out_shape=jax.ShapeDtypeStruct((b, nh, hd), F32),
        compiler_params=_params("arbitrary", "arbitrary"),
        name="fox_decode",
    )(pt_flat, q16, kn, vn, dexp, *([cache_k] * per_step), *([cache_v] * per_step))


def _moba_dec_kernel(pt_ref, slope_ref, q_ref, kn_ref, vn_ref, *refs, per_step, pages_per_block, nblocks):
    kp = refs[:per_step]
    vp = refs[per_step:2 * per_step]
    o_ref, s_ref, gate_ref, m_ref, l_ref, acc_ref = refs[2 * per_step:]
    g = pl.program_id(1)
    ng = pl.num_programs(1) // 2
    rows, nh, hd = kp[0].shape
    w = rows * nh
    npages = ng * per_step
    past = npages * rows
    q = q_ref[...]
    sub = lax.broadcasted_iota(jnp.int32, (nh, 1), 0)
    colw = lax.broadcasted_iota(jnp.int32, (nh, w), 1)
    own = (colw % nh) == lax.broadcasted_iota(jnp.int32, (nh, w), 0)
    lane = lax.broadcasted_iota(jnp.int32, (nh, LANES), 1)
    qf = q[:nh, :].astype(F32)
    s_self = jnp.sum(qf * kn_ref[...].astype(BF16).astype(F32), axis=-1, keepdims=True)

    @pl.when(g == 0)
    def _():
        gate_ref[...] = jnp.zeros_like(gate_ref)

    @pl.when(g < ng)
    def _():
        gate = gate_ref[...]
        for p in range(per_step):
            page = g * per_step + p
            s = _page_scores(q, kp[p][...])
            s_ref[:, pl.ds(pl.multiple_of(page * w, w), w)] = s
            tot = jnp.sum(jnp.where(own, s, 0.0), axis=-1, keepdims=True)
            gate = gate + jnp.where(lane == page // pages_per_block, tot, 0.0)
        gate_ref[...] = gate

    @pl.when(g == ng - 1)
    def _():
        gsel = jnp.where(lane < nblocks, gate_ref[...], -jnp.inf)
        chosen = jnp.zeros((nh, LANES), F32)
        for _ in range(MOBA_TOPK):
            mx = jnp.max(gsel, axis=-1, keepdims=True)
            first = jnp.min(jnp.where(gsel == mx, lane, LANES), axis=-1, keepdims=True)
            chosen = jnp.where(lane == first, 1.0, chosen)
            gsel = jnp.where(lane == first, -jnp.inf, gsel)
        slope = jnp.zeros((nh, 1), F32)
        for h in range(nh):
            slope = jnp.where(sub == h, slope_ref[h], slope)

        def biased(page):
            picked = jnp.sum(jnp.where(lane == page // pages_per_block, chosen, 0.0), axis=-1, keepdims=True)
            pos = page * rows + colw // nh
            s = s_ref[:, pl.ds(pl.multiple_of(page * w, w), w)] - slope * (past - pos).astype(F32)
            return jnp.where(own & (picked > 0.0), s, NEG)

        m = lax.fori_loop(0, npages,
                          lambda pg, m: jnp.maximum(m, jnp.max(biased(pg), axis=-1, keepdims=True)), s_self)

        def weights(pg, l):
            pr = jnp.exp(biased(pg) - m)
            s_ref[:, pl.ds(pl.multiple_of(pg * w, w), w)] = pr
            return l + jnp.sum(pr, axis=-1, keepdims=True)

        m_ref[...] = m
        l_ref[...] = lax.fori_loop(0, npages, weights, jnp.exp(s_self - m))
        acc_ref[...] = jnp.zeros_like(acc_ref)

    @pl.when(g >= ng)
    def _():
        acc = acc_ref[...]
        pad = jnp.zeros((nh, w), F32)
        for p in range(per_step):
            page = (g - ng) * per_step + p
            pr = s_ref[:, pl.ds(pl.multiple_of(page * w, w), w)]
            pb = jnp.concatenate([pr, pad], axis=0).astype(BF16)
            acc = acc + _dot(pb, vp[p][...].reshape(w, hd).astype(BF16))[:nh, :]
        acc_ref[...] = acc

    @pl.when(g == 2 * ng - 1)
    def _():
        p_self = jnp.exp(s_self - m_ref[...])
        acc = acc_ref[...] + p_self.astype(BF16).astype(F32) * vn_ref[...].astype(BF16).astype(F32)
        o_ref[...] = acc / l_ref[...]


def _moba_decode(q16, kn, vn, slopes, cache_k, cache_v, layer, pt_flat, npages, per_step):
    b = q16.shape[0]
    _, _, rows, nh, hd = cache_k.shape
    w = rows * nh
    ng = npages // per_step
    pages_per_block = MOBA_BLOCK // rows

    def page(p, value):
        def index(bi, g, pt_ref):
            step = jnp.maximum(g - ng, 0) if value else jnp.minimum(g, ng - 1)
            return (layer, pt_ref[bi * npages + step * per_step + p], 0, 0, 0)
        return pl.BlockSpec((None, None, rows, nh, hd), index)

    grid_spec = pltpu.PrefetchScalarGridSpec(
        num_scalar_prefetch=1,
        grid=(b, 2 * ng),
        in_specs=[pl.BlockSpec(memory_space=pltpu.SMEM),
                  pl.BlockSpec((None, 2 * SUBLANES, hd), lambda bi, g, pt: (bi, 0, 0)),
                  pl.BlockSpec((None, nh, hd), lambda bi, g, pt: (bi, 0, 0)),
                  pl.BlockSpec((None, nh, hd), lambda bi, g, pt: (bi, 0, 0))]
                 + [page(p, False) for p in range(per_step)] + [page(p, True) for p in range(per_step)],
        out_specs=pl.BlockSpec((None, nh, hd), lambda bi, g, pt: (bi, 0, 0)),
        scratch_shapes=[pltpu.VMEM((nh, npages * w), F32), pltpu.VMEM((nh, LANES), F32),
                        pltpu.VMEM((nh, 1), F32), pltpu.VMEM((nh, 1), F32), pltpu.VMEM((nh, hd), F32)],
    )
    return pl.pallas_call(
        functools.partial(_moba_dec_kernel, per_step=per_step, pages_per_block=pages_per_block,
                          nblocks=npages // pages_per_block),
        grid_spec=grid_spec,
        out_shape=jax.ShapeDtypeStruct((b, nh, hd), F32),
        compiler_params=_params("arbitrary", "arbitrary"),
        name="moba_decode",
    )(pt_flat, slopes, q16, kn, vn, *([cache_k] * per_step), *([cache_v] * per_step))


def _split_w_in(w_in, nh_fox, fw, mw, d):
    o1 = 3 * fw
    o2 = o1 + nh_fox
    o3 = o2 + 3 * mw
    wf = w_in[:, :o1].astype(BF16)
    wl = jnp.pad(w_in[:, o1:o2], ((0, 0), (0, LANES - nh_fox))).astype(BF16)
    wm = w_in[:, o2:o3].astype(BF16)
    wg = w_in[:, o3:].astype(BF16)
    return wf, wl, wm, wg


def kernel(x_prompt, x_sample, cache_fox_k, cache_fox_v, cache_fox_logf, cache_moba_k, cache_moba_v,
           state_conv, page_table, c_prompt, c_sample, w_ada, b_ada, g_mix_pre, g_mix_post, w_in, b_forget,
           w_branch_fox, w_branch_moba, w_out, g_ffn_pre, g_ffn_post, w_up, w_conv, b_conv, w_down):
    depth = w_ada.shape[0]
    bp, s, d = x_prompt.shape
    bs, dec_s, _ = x_sample.shape
    nh_fox = b_forget.shape[1]
    fw = w_branch_fox.shape[1]
    mw = w_branch_moba.shape[1]
    nh_moba = mw // HEAD_DIM
    dff = w_down.shape[1]
    n_phys, page_rows = cache_fox_k.shape[1], cache_fox_k.shape[2]
    npages = page_table.shape[1]
    past = npages * page_rows
    assert dec_s == 1 and fw == nh_fox * HEAD_DIM and nh_fox == SUBLANES and nh_moba == SUBLANES
    assert fw % LANES == 0 and mw % LANES == 0 and s % (SUBLANES * MOBA_BLOCK) == 0
    assert MOBA_BLOCK % page_rows == 0 and past % MOBA_BLOCK == 0 and past // MOBA_BLOCK >= MOBA_TOPK
    tm = 512
    per_step = 8
    assert s % tm == 0 and npages % per_step == 0
    nbp = -(-(s // MOBA_BLOCK) // LANES) * LANES
    slopes = jnp.asarray([2.0 ** (-8.0 * (h + 1) / nh_moba) for h in range(nh_moba)], F32)
    pt_flat = page_table.reshape(-1).astype(jnp.int32)

    rows_c = bp + bs
    rows_pad = -(-rows_c // SUBLANES) * SUBLANES
    c_all = jnp.pad(jnp.concatenate([c_prompt, c_sample], axis=0), ((0, rows_pad - rows_c), (0, 0)))

    y_p, y_s = x_prompt, x_sample.reshape(1, bs, d)
    outs_p, outs_s = [], []
    for l in range(depth):
        mod = _ada(c_all, w_ada[l], b_ada[l][None, :]).reshape(rows_pad, N_MOD, d)
        mod_p = [mod[:bp, i][:, None, :] for i in range(N_MOD)]
        mod_s = [mod[bp:rows_c, i][None] for i in range(N_MOD)]
        wf, wl, wm, wg = _split_w_in(w_in[l], nh_fox, fw, mw, d)
        bl = jnp.pad(b_forget[l], (0, LANES - nh_fox))[None, :]
        g1, g2 = g_mix_pre[l][None, :], g_mix_post[l][None, :]
        g3, g4 = g_ffn_pre[l][None, :], g_ffn_post[l][None, :]
        wbf, wbm, wo = w_branch_fox[l].astype(BF16), w_branch_moba[l].astype(BF16), w_out[l].astype(BF16)
        wup, wdn = w_up[l].astype(BF16), w_down[l].astype(BF16)
        wc, bc = w_conv[l], b_conv[l][None, :]

        (qf, kf, vf, kfb, vfb, lf, lft, qm, km, vm, kmb, vmb, sgf, sgm) = _inproj(
            y_p, mod_p[0], mod_p[1], g1, wf, wl, wm, wg, bl, tm=tm, nh=nh_fox)
        cum = _cumsum(lft)
        ck = cum.reshape(bp, nh_fox // 2, 2, s)
        cq = jnp.swapaxes(ck, 2, 3)
        o_f = _fox_attention(qf, kfb, vfb, cq, ck, t=tm)
        kmean = _kmean(km, nbp)
        o_m = _moba_attention(slopes, qm, kmb, vmb, kmean)
        x1 = _outproj(y_p, o_f, o_m, sgf, sgm, mod_p[2], g2, wbf, wbm, wo, tm=tm)
        y_p, tail = _ffn(x1, mod_p[3], mod_p[4], mod_p[5], g3, g4, wup, wc, bc, wdn, tm=tm)
        heads = lambda t, n: t.reshape(bp, s, n, HEAD_DIM)
        outs_p.append((heads(kf, nh_fox), heads(vf, nh_fox), lf, heads(km, nh_moba), heads(vm, nh_moba),
                       tail[:, SUBLANES - (CONV_W - 1):]))

        (qf, kf, vf, _, _, lf, _, qm, km, vm, _, _, sgf, sgm) = _inproj(
            y_s, mod_s[0], mod_s[1], g1, wf, wl, wm, wg, bl, tm=bs, nh=nh_fox)
        pad16 = lambda q, n: jnp.pad(q.reshape(bs, n, HEAD_DIM), ((0, 0), (0, 2 * SUBLANES - n), (0, 0)))
        hd3 = lambda t, n: t.reshape(bs, n, HEAD_DIM)
        logf_past = _gather_logf(cache_fox_logf, l, pt_flat, bs, npages, per_step)
        bias = _suffix_bias(jnp.swapaxes(logf_past, 1, 2), lf.reshape(bs, nh_fox, 1))
        dexp = jnp.repeat(bias, nh_fox, axis=2)
        o_f = _fox_decode(pad16(qf, nh_fox), hd3(kf, nh_fox), hd3(vf, nh_fox), dexp,
                          cache_fox_k, cache_fox_v, l, pt_flat, npages, per_step)
        o_m = _moba_decode(pad16(qm, nh_moba), hd3(km, nh_moba), hd3(vm, nh_moba), slopes,
                           cache_moba_k, cache_moba_v, l, pt_flat, npages, per_step)
        o_f = o_f.reshape(1, bs, fw).astype(BF16)
        o_m = o_m.reshape(1, bs, mw).astype(BF16)
        x1 = _outproj(y_s, o_f, o_m, sgf, sgm, mod_s[2], g2, wbf, wbm, wo, tm=bs)
        prev = (state_conv[l][None, :, 0], state_conv[l][None, :, 1])
        y_s, u_new = _ffn(x1, mod_s[3], mod_s[4], mod_s[5], g3, g4, wup, wc, bc, wdn, prev, tm=bs)
        heads = lambda t, n: t.reshape(bs, 1, n, HEAD_DIM)
        outs_s.append((heads(kf, nh_fox), heads(vf, nh_fox), lf.reshape(bs, 1, nh_fox),
                       heads(km, nh_moba), heads(vm, nh_moba),
                       jnp.stack([state_conv[l][:, 1], u_new[0]], axis=1)))

    stack = lambda rows, i: jnp.stack([r[i] for r in rows])
    return (y_p, y_s.reshape(bs, 1, d),
            stack(outs_p, 0), stack(outs_p, 1), stack(outs_p, 2), stack(outs_p, 3), stack(outs_p, 4), stack(outs_p, 5),
            stack(outs_s, 0), stack(outs_s, 1), stack(outs_s, 2), stack(outs_s, 3), stack(outs_s, 4), stack(outs_s, 5))
```

```python
import functools

import jax
import jax.numpy as jnp
from jax import lax
from jax.experimental import pallas as pl
from jax.experimental.pallas import tpu as pltpu

F32 = jnp.float32
BF16 = jnp.bfloat16

HEAD_DIM = 64
MOBA_BLOCK = 256
MOBA_TOPK = 3
CONV_W = 3
N_MOD = 6
EPS = 1e-6
LANES = 128
SUBLANES = 8
LOG2E = 1.4426950408889634
BIG = 2.0 ** 100
VMEM_LIMIT = 56 * 1024 * 1024

SEL_LANES = 64
A_BLK = 64
A_OFF = 67
A_QPOS = 70
A_END = 73

_NT = (((1,), (1,)), ((), ()))


def _params(*sem):
    return pltpu.CompilerParams(dimension_semantics=sem, vmem_limit_bytes=VMEM_LIMIT)


def _const_spec(shape):
    nd = len(shape)
    return pl.BlockSpec(shape, lambda *_: (0,) * nd, pipeline_mode=pl.Buffered(1))


def _rms(x, g):
    return x * lax.rsqrt(jnp.mean(x * x, axis=-1, keepdims=True) + EPS) * g


def _dot(a, b):
    return jnp.dot(a, b, preferred_element_type=F32)


def _dot_nt(a, b):
    return lax.dot_general(a, b, _NT, preferred_element_type=F32)


def _split3(x):
    p1 = x.astype(BF16).astype(F32)
    r = x - p1
    p2 = r.astype(BF16).astype(F32)
    p3 = (r - p2).astype(BF16).astype(F32)
    return p1, p2, p3


def _ada_kernel(c_ref, w_ref, b_ref, o_ref):
    c = c_ref[...]
    s = (c * jax.nn.sigmoid(c)).astype(BF16)
    o_ref[...] = _dot(s, w_ref[...].astype(BF16)) + b_ref[...]


def _ada(c, w, b):
    rows, d = c.shape
    n = w.shape[1]
    tn = 1536 if n % 1536 == 0 else n
    return pl.pallas_call(
        _ada_kernel,
        grid=(n // tn,),
        in_specs=[pl.BlockSpec((rows, d), lambda j: (0, 0)),
                  pl.BlockSpec((d, tn), lambda j: (0, j)),
                  pl.BlockSpec((1, tn), lambda j: (0, j))],
        out_specs=pl.BlockSpec((rows, tn), lambda j: (0, j)),
        out_shape=jax.ShapeDtypeStruct((rows, n), F32),
        compiler_params=_params("arbitrary"),
        name="ada_mod",
    )(c, w, b)


def _inproj_kernel(x_ref, shift_ref, scale_ref, g_ref, wf_ref, wl_ref, wm_ref, wg_ref, bl_ref,
                   qf_ref, kf_ref, vf_ref, kfb_ref, vfb_ref, lf_ref, lft_ref,
                   qm_ref, km_ref, vm_ref, kmb_ref, vmb_ref, sgf_ref, sgm_ref, *, fw, mw, d, nh):
    x = x_ref[...]
    h = (_rms(x, g_ref[...]) * (1.0 + scale_ref[...]) + shift_ref[...]).astype(BF16)
    qs = LOG2E * HEAD_DIM ** -0.5

    a = _dot(h, wf_ref[...])
    qf_ref[...] = (a[:, :fw] * qs).astype(BF16)
    kf = a[:, fw:2 * fw]
    vf = a[:, 2 * fw:]
    kf_ref[...] = kf
    vf_ref[...] = vf
    kfb_ref[...] = kf.astype(BF16)
    vfb_ref[...] = vf.astype(BF16)

    z = _dot(h, wl_ref[...]) + bl_ref[...]
    lf = jnp.minimum(z, 0.0) - jnp.log1p(jnp.exp(-jnp.abs(z)))
    lf_ref[...] = lf[:, :nh]
    lft_ref[...] = lf.T[:nh, :]

    a = _dot(h, wm_ref[...])
    qm_ref[...] = (a[:, :mw] * qs).astype(BF16)
    km = a[:, mw:2 * mw]
    vm = a[:, 2 * mw:]
    km_ref[...] = km
    vm_ref[...] = vm
    kmb_ref[...] = km.astype(BF16)
    vmb_ref[...] = vm.astype(BF16)

    a = jax.nn.sigmoid(_dot(h, wg_ref[...]))
    sgf_ref[...] = a[:, :d].astype(BF16)
    sgm_ref[...] = a[:, d:].astype(BF16)


def _inproj(x, shift, scale, g, wf, wl, wm, wg, bl, *, tm, nh):
    b, s, d = x.shape
    fw = wf.shape[1] // 3
    mw = wm.shape[1] // 3
    mrows = shift.shape[1]
    if mrows == 1:
        mod_spec = pl.BlockSpec((None, 1, d), lambda bi, i: (bi, 0, 0))
    else:
        mod_spec = pl.BlockSpec((None, tm, d), lambda bi, i: (bi, i, 0))
    row = lambda w: pl.BlockSpec((None, tm, w), lambda bi, i: (bi, i, 0))
    outs = [(fw, BF16), (fw, F32), (fw, F32), (fw, BF16), (fw, BF16), (nh, F32), None,
            (mw, BF16), (mw, F32), (mw, F32), (mw, BF16), (mw, BF16), (d, BF16), (d, BF16)]
    out_shape, out_specs = [], []
    for o in outs:
        if o is None:
            out_shape.append(jax.ShapeDtypeStruct((b, nh, s), F32))
            out_specs.append(pl.BlockSpec((None, nh, tm), lambda bi, i: (bi, 0, i)))
        else:
            out_shape.append(jax.ShapeDtypeStruct((b, s, o[0]), o[1]))
            out_specs.append(row(o[0]))
    return pl.pallas_call(
        functools.partial(_inproj_kernel, fw=fw, mw=mw, d=d, nh=nh),
        grid=(b, s // tm),
        in_specs=[row(d), mod_spec, mod_spec, _const_spec((1, d)),
                  _const_spec(wf.shape), _const_spec(wl.shape), _const_spec(wm.shape),
                  _const_spec(wg.shape), _const_spec((1, LANES))],
        out_specs=out_specs,
        out_shape=out_shape,
        compiler_params=_params("arbitrary", "arbitrary"),
        name="in_proj",
    )(x, shift, scale, g, wf, wl, wm, wg, bl)


def _scan_lanes(x, reverse):
    n = x.shape[-1]
    lane = lax.broadcasted_iota(jnp.int32, x.shape, x.ndim - 1)
    sh = 1
    while sh < n:
        if reverse:
            x = x + jnp.where(lane < n - sh, pltpu.roll(x, n - sh, axis=x.ndim - 1), 0.0)
        else:
            x = x + jnp.where(lane >= sh, pltpu.roll(x, sh, axis=x.ndim - 1), 0.0)
        sh *= 2
    return x


def _cumsum_kernel(x_ref, o_ref):
    o_ref[...] = _scan_lanes(x_ref[...], reverse=False)


def _cumsum(x):
    b, nh, s = x.shape
    return pl.pallas_call(
        _cumsum_kernel,
        grid=(b,),
        in_specs=[pl.BlockSpec((None, nh, s), lambda bi: (bi, 0, 0))],
        out_specs=pl.BlockSpec((None, nh, s), lambda bi: (bi, 0, 0)),
        out_shape=jax.ShapeDtypeStruct((b, nh, s), F32),
        compiler_params=_params("arbitrary"),
        name="logf_cumsum",
    )(x)


def _fox_aug_kernel(c_ref, aq_ref, ak_ref):
    c = c_ref[...] * LOG2E
    nh, tm = c.shape
    pieces = _split3(c)
    sub = lax.broadcasted_iota(jnp.int32, (LANES, tm), 0)
    row = lambda n, h: jnp.broadcast_to(pieces[n][h:h + 1, :], (LANES, tm))
    for h in range(nh):
        o = SUBLANES * (h % 2)
        mq = jnp.where((sub >= o + 3) & (sub < o + 6), 1.0, 0.0)
        for n in range(3):
            mq = jnp.where(sub == o + n, row(n, h), mq)
        aq_ref[:, h * LANES:(h + 1) * LANES] = mq.T.astype(BF16)
    for hp in range(nh // 2):
        mk = jnp.zeros((LANES, tm), F32)
        for hh in range(2):
            o = SUBLANES * hh
            mk = jnp.where((sub >= o) & (sub < o + 3), 1.0, mk)
            for n in range(3):
                mk = jnp.where(sub == o + 3 + n, -row(n, 2 * hp + hh), mk)
        ak_ref[:, hp * LANES:(hp + 1) * LANES] = mk.T.astype(BF16)


def _fox_aug(cum, *, tm):
    b, nh, s = cum.shape
    return pl.pallas_call(
        _fox_aug_kernel,
        grid=(b, s // tm),
        in_specs=[pl.BlockSpec((None, nh, tm), lambda bi, i: (bi, 0, i))],
        out_specs=[pl.BlockSpec((None, tm, nh * LANES), lambda bi, i: (bi, i, 0)),
                   pl.BlockSpec((None, tm, nh // 2 * LANES), lambda bi, i: (bi, i, 0))],
        out_shape=[jax.ShapeDtypeStruct((b, s, nh * LANES), BF16),
                   jax.ShapeDtypeStruct((b, s, nh // 2 * LANES), BF16)],
        compiler_params=_params("arbitrary", "arbitrary"),
        name="fox_aug",
    )(cum)


def _kmean_kernel(k_ref, o_ref, *, nvalid):
    i = pl.program_id(1)

    @pl.when(i < nvalid)
    def _():
        k = k_ref[...]
        o_ref[...] = jnp.mean(k.reshape(SUBLANES, MOBA_BLOCK, k.shape[-1]), axis=1)

    @pl.when(i >= nvalid)
    def _():
        o_ref[...] = jnp.zeros_like(o_ref)


def _kmean(k, nbp):
    b, s, w = k.shape
    rows = SUBLANES * MOBA_BLOCK
    nvalid = s // rows
    return pl.pallas_call(
        functools.partial(_kmean_kernel, nvalid=nvalid),
        grid=(b, nbp // SUBLANES),
        in_specs=[pl.BlockSpec((None, rows, w), lambda bi, i: (bi, jnp.minimum(i, nvalid - 1), 0))],
        out_specs=pl.BlockSpec((None, SUBLANES, w), lambda bi, i: (bi, i, 0)),
        out_shape=jax.ShapeDtypeStruct((b, nbp, w), F32),
        compiler_params=_params("arbitrary", "arbitrary"),
        name="moba_kmean",
    )(k)


def _pick_topk(g, idx, nbp):
    sel = jnp.zeros(g.shape, jnp.bool_)
    for _ in range(MOBA_TOPK):
        mx = jnp.max(g, axis=-1, keepdims=True)
        is_max = (g == mx) & (mx > -jnp.inf)
        first = jnp.min(jnp.where(is_max, idx, nbp), axis=-1, keepdims=True)
        pick = idx == first
        sel = sel | pick
        g = jnp.where(pick, -jnp.inf, g)
    return sel


def _moba_aug_kernel(slope_ref, q_ref, km_ref, aq_ref):
    t = MOBA_BLOCK
    hp = pl.program_id(1)
    i = pl.program_id(2)
    q2 = q_ref[...]
    km = km_ref[...]
    km_hi = km.astype(BF16)
    km_lo = (km - km_hi.astype(F32)).astype(BF16)
    lane = lax.broadcasted_iota(jnp.int32, (1, LANES), 1)
    blk = lax.broadcasted_iota(jnp.int32, (t, LANES), 1)
    qpos = (i * t + lax.broadcasted_iota(jnp.int32, (t, LANES), 0)).astype(F32)
    zero = jnp.zeros_like(q2)
    for hh in range(2):
        sig = jnp.full((t, LANES), slope_ref[2 * hp + hh] * LOG2E, F32)
        qh = jnp.where((lane < HEAD_DIM) if hh == 0 else (lane >= HEAD_DIM), q2, zero)
        gate = _dot_nt(qh, km_hi) + _dot_nt(qh, km_lo)
        sel = _pick_topk(jnp.where(blk < i, gate, -jnp.inf), blk, LANES)
        aug = jnp.where(sel | (blk == i), 0.0, -BIG)
        s3 = _split3(sig)
        t3 = _split3(-sig * qpos)
        for n in range(3):
            aug = jnp.where((blk == A_BLK + n) | (blk == A_OFF + n), s3[n], aug)
            aug = jnp.where(blk == A_QPOS + n, t3[n], aug)
        aug = jnp.where(blk >= A_END, 0.0, aug)
        aq_ref[:, hh * LANES:(hh + 1) * LANES] = aug.astype(BF16)


def _moba_aug(slopes, q, kmean):
    b, s, w = q.shape
    npair = w // LANES
    t = MOBA_BLOCK
    return pl.pallas_call(
        _moba_aug_kernel,
        grid=(b, npair, s // t),
        in_specs=[pl.BlockSpec(memory_space=pltpu.SMEM),
                  pl.BlockSpec((None, t, LANES), lambda bi, hp, i: (bi, i, hp)),
                  pl.BlockSpec((None, LANES, LANES), lambda bi, hp, i: (bi, 0, hp))],
        out_specs=pl.BlockSpec((None, t, 2 * LANES), lambda bi, hp, i: (bi, i, hp)),
        out_shape=jax.ShapeDtypeStruct((b, s, 2 * w), BF16),
        compiler_params=_params("arbitrary", "arbitrary", "arbitrary"),
        name="moba_aug",
    )(slopes, q, kmean)


def _moba_key_table(s):
    pos = jnp.arange(s, dtype=jnp.int32)[:, None]
    lane = jnp.arange(LANES, dtype=jnp.int32)[None, :]
    blk = pos // MOBA_BLOCK
    tab = jnp.where(lane == blk, 1, 0)
    tab = jnp.where((lane >= A_BLK) & (lane < A_OFF), MOBA_BLOCK * blk, tab)
    tab = jnp.where((lane >= A_OFF) & (lane < A_QPOS), pos % MOBA_BLOCK, tab)
    tab = jnp.where((lane >= A_QPOS) & (lane < A_END), 1, tab)
    return tab.astype(BF16)[None]


def _attn_kernel(q_ref, aq_ref, k_ref, ak_ref, v_ref, o_ref, *, t, block_causal):
    i = pl.program_id(2)
    q2 = q_ref[...]
    lane = lax.broadcasted_iota(jnp.int32, (1, LANES), 1)
    zero = jnp.zeros_like(q2)
    qa = (jnp.concatenate([jnp.where(lane < HEAD_DIM, q2, zero), aq_ref[:, :LANES]], axis=1),
          jnp.concatenate([jnp.where(lane >= HEAD_DIM, q2, zero), aq_ref[:, LANES:]], axis=1))
    ones = jnp.ones((t, LANES), BF16)

    def step(s, va, m, acc):
        part = s[:, :LANES]
        for c in range(1, t // LANES):
            part = jnp.maximum(part, s[:, c * LANES:(c + 1) * LANES])
        m_new = jnp.maximum(m, jnp.max(part, axis=-1, keepdims=True))
        p = jnp.exp2(s - m_new)
        acc = jnp.exp2(m - m_new) * acc + _dot(p.astype(BF16), va)
        return m_new, acc

    def tiles(j):
        ks = pl.multiple_of(j * t, t)
        ka = jnp.concatenate([k_ref[pl.ds(ks, t), :], ak_ref[pl.ds(ks, t), :]], axis=1)
        va = jnp.concatenate([v_ref[pl.ds(ks, t), :], ones], axis=1)
        return ka, va

    def body(j, carry):
        ka, va = tiles(j)
        return tuple(step(_dot_nt(qa[hh], ka), va, *carry[hh]) for hh in range(2))

    init = (jnp.full((t, 1), -jnp.inf, F32), jnp.zeros((t, 2 * LANES), F32))
    carry = lax.fori_loop(0, i, body, (init, init))

    ka, va = tiles(i)
    row = lax.broadcasted_iota(jnp.int32, (t, t), 0)
    col = lax.broadcasted_iota(jnp.int32, (t, t), 1)
    future = col > row
    if block_causal:
        future = future & ((col // MOBA_BLOCK) == (row // MOBA_BLOCK))
    outs = []
    for hh in range(2):
        s = jnp.where(future, -jnp.inf, _dot_nt(qa[hh], ka))
        _, acc = step(s, va, *carry[hh])
        outs.append(acc[:, :LANES] / acc[:, LANES:])
    o_ref[...] = jnp.where(lane < HEAD_DIM, outs[0], outs[1]).astype(BF16)


def _attention(q, aq, k, ak, v, *, t, block_causal, name):
    b, s, w = q.shape
    npair = w // LANES
    per_seq = ak.shape[0] > 1
    per_pair = ak.shape[2] > LANES
    return pl.pallas_call(
        functools.partial(_attn_kernel, t=t, block_causal=block_causal),
        grid=(b, npair, s // t),
        in_specs=[pl.BlockSpec((None, t, LANES), lambda bi, hp, i: (bi, i, hp)),
                  pl.BlockSpec((None, t, 2 * LANES), lambda bi, hp, i: (bi, i, hp)),
                  pl.BlockSpec((None, s, LANES), lambda bi, hp, i: (bi, 0, hp)),
                  pl.BlockSpec((None, s, LANES),
                               lambda bi, hp, i: (bi if per_seq else 0, 0, hp if per_pair else 0)),
                  pl.BlockSpec((None, s, LANES), lambda bi, hp, i: (bi, 0, hp))],
        out_specs=pl.BlockSpec((None, t, LANES), lambda bi, hp, i: (bi, i, hp)),
        out_shape=jax.ShapeDtypeStruct((b, s, w), BF16),
        compiler_params=_params("arbitrary", "arbitrary", "arbitrary"),
        name=name,
    )(q, aq, k, ak, v)


def _outproj_kernel(x_ref, of_ref, om_ref, sgf_ref, sgm_ref, gate_ref, g_ref, wbf_ref, wbm_ref, wo_ref, y_ref):
    yf = _dot(of_ref[...], wbf_ref[...])
    ym = _dot(om_ref[...], wbm_ref[...])
    mixed = sgf_ref[...].astype(F32) * yf + sgm_ref[...].astype(F32) * ym
    z = _dot(mixed.astype(BF16), wo_ref[...])
    y_ref[...] = x_ref[...] + gate_ref[...] * _rms(z, g_ref[...])


def _outproj(x, of, om, sgf, sgm, gate, g, wbf, wbm, wo, *, tm):
    b, s, d = x.shape
    if gate.shape[1] == 1:
        mod_spec = pl.BlockSpec((None, 1, d), lambda bi, i: (bi, 0, 0))
    else:
        mod_spec = pl.BlockSpec((None, tm, d), lambda bi, i: (bi, i, 0))
    row = lambda w: pl.BlockSpec((None, tm, w), lambda bi, i: (bi, i, 0))
    return pl.pallas_call(
        _outproj_kernel,
        grid=(b, s // tm),
        in_specs=[row(d), row(of.shape[-1]), row(om.shape[-1]), row(d), row(d), mod_spec,
                  _const_spec((1, d)), _const_spec(wbf.shape), _const_spec(wbm.shape), _const_spec(wo.shape)],
        out_specs=row(d),
        out_shape=jax.ShapeDtypeStruct((b, s, d), F32),
        compiler_params=_params("arbitrary", "arbitrary"),
        name="out_proj",
    )(x, of, om, sgf, sgm, gate, g, wbf, wbm, wo)


def _gelu_tanh(x):
    c = 0.7978845608028654
    return 0.5 * x * (1.0 + jnp.tanh(c * (x + 0.044715 * (x * x * x))))


def _ffn_kernel(*refs, dff, ch, tm, stepwise):
    if stepwise:
        (x_ref, shift_ref, scale_ref, gate_ref, gpre_ref, gpost_ref, wup_ref, wc_ref, bc_ref, wdn_ref,
         p2_ref, p1_ref, y_ref, u_ref, acc_ref) = refs
    else:
        (x_ref, shift_ref, scale_ref, gate_ref, gpre_ref, gpost_ref, wup_ref, wc_ref, bc_ref, wdn_ref,
         y_ref, tail_ref, acc_ref, ubuf_ref, carry_ref) = refs
        first = pl.program_id(1) == 0
    x = x_ref[...]
    h = (_rms(x, gpre_ref[...]) * (1.0 + scale_ref[...]) + shift_ref[...]).astype(BF16)
    acc_ref[...] = jnp.zeros_like(acc_ref)
    for c in range(dff // ch):
        cs = slice(c * ch, (c + 1) * ch)
        u = _dot(h, wup_ref[:, cs])
        v = _dot(h, wup_ref[:, dff + c * ch:dff + (c + 1) * ch])
        if stepwise:
            u2 = p2_ref[:, cs]
            u1 = p1_ref[:, cs]
            u_ref[:, cs] = u
        else:
            @pl.when(first)
            def _():
                carry_ref[c] = jnp.zeros((SUBLANES, ch), F32)

            ubuf_ref[0:SUBLANES, :] = carry_ref[c]
            ubuf_ref[SUBLANES:SUBLANES + tm, :] = u
            u1 = ubuf_ref[SUBLANES - 1:SUBLANES - 1 + tm, :]
            u2 = ubuf_ref[SUBLANES - 2:SUBLANES - 2 + tm, :]
            tail = u[tm - SUBLANES:, :]
            carry_ref[c] = tail
            tail_ref[:, cs] = tail
        uc = bc_ref[:, cs] + (wc_ref[0:1, cs] * u2 + wc_ref[1:2, cs] * u1 + wc_ref[2:3, cs] * u)
        g = (_gelu_tanh(uc) * v).astype(BF16)
        acc_ref[...] += _dot(g, wdn_ref[cs, :])
    y_ref[...] = x + gate_ref[...] * _rms(acc_ref[...], gpost_ref[...])


def _ffn(x, shift, scale, gate, gpre, gpost, wup, wc, bc, wdn, prev=None, *, tm):
    b, s, d = x.shape
    dff = wdn.shape[0]
    ch = 256 if dff % 256 == 0 else LANES
    stepwise = prev is not None
    if shift.shape[1] == 1:
        mod_spec = pl.BlockSpec((None, 1, d), lambda bi, i: (bi, 0, 0))
    else:
        mod_spec = pl.BlockSpec((None, tm, d), lambda bi, i: (bi, i, 0))
    row = lambda w: pl.BlockSpec((None, tm, w), lambda bi, i: (bi, i, 0))
    in_specs = [row(d), mod_spec, mod_spec, mod_spec, _const_spec((1, d)), _const_spec((1, d)),
                _const_spec(wup.shape), _const_spec(wc.shape), _const_spec(bc.shape), _const_spec(wdn.shape)]
    args = [x, shift, scale, gate, gpre, gpost, wup, wc, bc, wdn]
    scratch = [pltpu.VMEM((tm, d), F32)]
    if stepwise:
        in_specs += [row(dff), row(dff)]
        args += list(prev)
        out_specs = [row(d), row(dff)]
        out_shape = [jax.ShapeDtypeStruct((b, s, d), F32), jax.ShapeDtypeStruct((b, s, dff), F32)]
    else:
        out_specs = [row(d), pl.BlockSpec((None, SUBLANES, dff), lambda bi, i: (bi, 0, 0))]
        out_shape = [jax.ShapeDtypeStruct((b, s, d), F32), jax.ShapeDtypeStruct((b, SUBLANES, dff), F32)]
        scratch += [pltpu.VMEM((tm + SUBLANES, ch), F32), pltpu.VMEM((dff // ch, SUBLANES, ch), F32)]
    return pl.pallas_call(
        functools.partial(_ffn_kernel, dff=dff, ch=ch, tm=tm, stepwise=stepwise),
        grid=(b, s // tm),
        in_specs=in_specs,
        out_specs=out_specs,
        out_shape=out_shape,
        scratch_shapes=scratch,
        compiler_params=_params("arbitrary", "arbitrary"),
        name="ffn_step" if stepwise else "ffn_seq",
    )(*args)


def _page_spec(block, layer, npages_per_seq, per_step, p, step_of=lambda g: g):
    def index(bi, g, pt_ref):
        page = pt_ref[bi * npages_per_seq + step_of(g) * per_step + p]
        return (layer, page) + (0,) * (len(block) - 2)
    return pl.BlockSpec(block, index)


def _col_spec(nh, hd):
    return pl.BlockSpec((None, nh, hd, 1), lambda bi, g, pt: (bi, 0, 0, 0))


def _stack_rows(rows):
    shape = (len(rows), rows[0].shape[1])
    sub = lax.broadcasted_iota(jnp.int32, shape, 0)
    out = jnp.broadcast_to(rows[0], shape)
    for h in range(1, len(rows)):
        out = jnp.where(sub == h, jnp.broadcast_to(rows[h], shape), out)
    return out


def _page_scores(k_ref, qb_ref):
    nh = k_ref.shape[0]
    return _stack_rows([jnp.sum(k_ref[h] * qb_ref[h], axis=0, keepdims=True) for h in range(nh)])


def _logf_bias_kernel(pt_ref, new_ref, *refs, per_step):
    pages = refs[:per_step]
    o_ref, x_ref = refs[per_step:]
    g = pl.program_id(1)
    rows = pages[0].shape[-1]
    for p in range(per_step):
        x_ref[:, pl.ds(pl.multiple_of((g * per_step + p) * rows, rows), rows)] = pages[p][...]

    @pl.when(g == pl.num_programs(1) - 1)
    def _():
        x = x_ref[...]
        n = x.shape[-1]
        lane = lax.broadcasted_iota(jnp.int32, x.shape, 1)
        shifted = jnp.where(lane < n - 1, pltpu.roll(x, n - 1, axis=1), 0.0)
        o_ref[...] = (_scan_lanes(shifted, reverse=True) + new_ref[...]) * LOG2E


def _logf_bias(cache_t, layer, new, pt_flat, npages, per_step):
    _, _, nh, rows = cache_t.shape
    b = new.shape[0]
    n = npages * rows
    grid_spec = pltpu.PrefetchScalarGridSpec(
        num_scalar_prefetch=1,
        grid=(b, npages // per_step),
        in_specs=[pl.BlockSpec((None, nh, 1), lambda bi, g, pt: (bi, 0, 0))]
                 + [_page_spec((None, None, nh, rows), layer, npages, per_step, p) for p in range(per_step)],
        out_specs=pl.BlockSpec((None, nh, n), lambda bi, g, pt: (bi, 0, 0)),
        scratch_shapes=[pltpu.VMEM((nh, n), F32)],
    )
    return pl.pallas_call(
        functools.partial(_logf_bias_kernel, per_step=per_step),
        grid_spec=grid_spec,
        out_shape=jax.ShapeDtypeStruct((b, nh, n), F32),
        compiler_params=_params("arbitrary", "arbitrary"),
        name="logf_bias",
    )(pt_flat, new, *([cache_t] * per_step))


def _fox_dec_kernel(pt_ref, q_ref, kn_ref, vn_ref, d_ref, *refs, per_step):
    kp = refs[:per_step]
    vp = refs[per_step:2 * per_step]
    o_ref, qb_ref, m_ref, l_ref, acc_ref = refs[2 * per_step:]
    g = pl.program_id(1)
    nh, hd, rows = kp[0].shape

    @pl.when(g == 0)
    def _():
        for h in range(nh):
            qb_ref[h] = jnp.broadcast_to(q_ref[h], (hd, rows))
        m_ref[...] = jnp.full_like(m_ref, -BIG)
        l_ref[...] = jnp.zeros_like(l_ref)
        acc_ref[...] = jnp.zeros_like(acc_ref)

    ss = []
    m_new = m_ref[...]
    for p in range(per_step):
        s = _page_scores(kp[p], qb_ref) + d_ref[:, p * rows:(p + 1) * rows]
        ss.append(s)
        m_new = jnp.maximum(m_new, jnp.max(s, axis=-1, keepdims=True))
    alpha = jnp.exp2(m_ref[...] - m_new)
    l = alpha * l_ref[...]
    prs = []
    for p in range(per_step):
        pr = jnp.exp2(ss[p] - m_new)
        prs.append(pr)
        l = l + jnp.sum(pr, axis=-1, keepdims=True)
    for h in range(nh):
        acc = alpha[h:h + 1, :] * acc_ref[h]
        for p in range(per_step):
            acc = acc + vp[p][h] * prs[p][h:h + 1, :]
        acc_ref[h] = acc
    m_ref[...] = m_new
    l_ref[...] = l

    @pl.when(g == pl.num_programs(1) - 1)
    def _():
        for h in range(nh):
            s_self = jnp.sum(q_ref[h] * kn_ref[h], axis=0, keepdims=True)
            m_h = m_new[h:h + 1, :]
            m_fin = jnp.maximum(m_h, s_self)
            a = jnp.exp2(m_h - m_fin)
            p_self = jnp.exp2(s_self - m_fin)
            num = a * jnp.sum(acc_ref[h], axis=-1, keepdims=True) + p_self * vn_ref[h]
            o_ref[h] = num / (a * l[h:h + 1, :] + p_self)


def _fox_decode(qcol, kn, vn, bias, cache_k, cache_v, layer, pt_flat, npages, per_step):
    b = qcol.shape[0]
    _, _, nh, hd, rows = cache_k.shape
    page = lambda p: _page_spec((None, None, nh, hd, rows), layer, npages, per_step, p)
    grid_spec = pltpu.PrefetchScalarGridSpec(
        num_scalar_prefetch=1,
        grid=(b, npages // per_step),
        in_specs=[_col_spec(nh, hd), _col_spec(nh, hd), _col_spec(nh, hd),
                  pl.BlockSpec((None, nh, per_step * rows), lambda bi, g, pt: (bi, 0, g))]
                 + [page(p) for p in range(per_step)] + [page(p) for p in range(per_step)],
        out_specs=_col_spec(nh, hd),
        scratch_shapes=[pltpu.VMEM((nh, hd, rows), F32), pltpu.VMEM((nh, 1), F32), pltpu.VMEM((nh, 1), F32),
                        pltpu.VMEM((nh, hd, rows), F32)],
    )
    return pl.pallas_call(
        functools.partial(_fox_dec_kernel, per_step=per_step),
        grid_spec=grid_spec,
        out_shape=jax.ShapeDtypeStruct((b, nh, hd, 1), F32),
        compiler_params=_params("arbitrary", "arbitrary"),
        name="fox_decode",
    )(pt_flat, qcol, kn, vn, bias, *([cache_k] * per_step), *([cache_v] * per_step))


def _moba_dec_kernel(pt_ref, slope_ref, q_ref, kn_ref, vn_ref, *refs, per_step, pages_per_block, nblocks):
    kp = refs[:per_step]
    vp = refs[per_step:2 * per_step]
    o_ref, qb_ref, s_ref, gate_ref, m_ref, l_ref, acc_ref = refs[2 * per_step:]
    g = pl.program_id(1)
    ng = pl.num_programs(1) // 2
    nh, hd, rows = kp[0].shape
    npages = nblocks * pages_per_block
    past = npages * rows
    lane = lax.broadcasted_iota(jnp.int32, (nh, LANES), 1)

    @pl.when(g == 0)
    def _():
        for h in range(nh):
            qb_ref[h] = jnp.broadcast_to(q_ref[h], (hd, rows))
        gate_ref[...] = jnp.zeros_like(gate_ref)

    @pl.when(g < ng)
    def _():
        gate = gate_ref[...]
        for p in range(per_step):
            page = g * per_step + p
            s = _page_scores(kp[p], qb_ref)
            s_ref[:, pl.ds(pl.multiple_of(page * rows, rows), rows)] = s
            gate = gate + jnp.where(lane == page // pages_per_block, jnp.sum(s, axis=-1, keepdims=True), 0.0)
        gate_ref[...] = gate

    @pl.when(g == ng - 1)
    def _():
        gsel = jnp.where(lane < nblocks, gate_ref[...], -jnp.inf)
        chosen = jnp.zeros((nh, LANES), F32)
        for _ in range(MOBA_TOPK):
            mx = jnp.max(gsel, axis=-1, keepdims=True)
            first = jnp.min(jnp.where(gsel == mx, lane, LANES), axis=-1, keepdims=True)
            chosen = jnp.where(lane == first, 1.0, chosen)
            gsel = jnp.where(lane == first, -jnp.inf, gsel)
        sub = lax.broadcasted_iota(jnp.int32, (nh, 1), 0)
        sig = jnp.zeros((nh, 1), F32)
        for h in range(nh):
            sig = jnp.where(sub == h, slope_ref[h] * LOG2E, sig)
        picked = jnp.concatenate(
            [jnp.broadcast_to(jnp.sum(jnp.where(lane == n, chosen, 0.0), axis=-1, keepdims=True),
                              (nh, MOBA_BLOCK)) for n in range(nblocks)], axis=1)
        pos = lax.broadcasted_iota(jnp.int32, (nh, past), 1)
        sb = jnp.where(picked > 0.0, s_ref[...] - sig * (past - pos).astype(F32), -BIG)
        s_self = _stack_rows([jnp.sum(q_ref[h] * kn_ref[h], axis=0, keepdims=True) for h in range(nh)])
        m = jnp.maximum(jnp.max(sb, axis=-1, keepdims=True), s_self)
        pr = jnp.exp2(sb - m)
        s_ref[...] = pr
        m_ref[...] = jnp.exp2(s_self - m)
        l_ref[...] = jnp.sum(pr, axis=-1, keepdims=True) + m_ref[...]
        acc_ref[...] = jnp.zeros_like(acc_ref)

    @pl.when(g >= ng)
    def _():
        for h in range(nh):
            acc = acc_ref[h]
            for p in range(per_step):
                page = (g - ng) * per_step + p
                pr = s_ref[pl.ds(h, 1), pl.ds(pl.multiple_of(page * rows, rows), rows)]
                acc = acc + vp[p][h] * pr
            acc_ref[h] = acc

    @pl.when(g == 2 * ng - 1)
    def _():
        for h in range(nh):
            num = jnp.sum(acc_ref[h], axis=-1, keepdims=True) + m_ref[h:h + 1, :] * vn_ref[h]
            o_ref[h] = num / l_ref[h:h + 1, :]


def _moba_decode(qcol, kn, vn, slopes, cache_k, cache_v, layer, pt_flat, npages, per_step):
    b = qcol.shape[0]
    _, _, nh, hd, rows = cache_k.shape
    ng = npages // per_step
    pages_per_block = MOBA_BLOCK // rows
    kpage = lambda p: _page_spec((None, None, nh, hd, rows), layer, npages, per_step, p,
                                 step_of=lambda g: jnp.minimum(g, ng - 1))
    vpage = lambda p: _page_spec((None, None, nh, hd, rows), layer, npages, per_step, p,
                                 step_of=lambda g: jnp.maximum(g - ng, 0))
    grid_spec = pltpu.PrefetchScalarGridSpec(
        num_scalar_prefetch=1,
        grid=(b, 2 * ng),
        in_specs=[pl.BlockSpec(memory_space=pltpu.SMEM), _col_spec(nh, hd), _col_spec(nh, hd), _col_spec(nh, hd)]
                 + [kpage(p) for p in range(per_step)] + [vpage(p) for p in range(per_step)],
        out_specs=_col_spec(nh, hd),
        scratch_shapes=[pltpu.VMEM((nh, hd, rows), F32), pltpu.VMEM((nh, npages * rows), F32),
                        pltpu.VMEM((nh, LANES), F32), pltpu.VMEM((nh, 1), F32), pltpu.VMEM((nh, 1), F32),
                        pltpu.VMEM((nh, hd, rows), F32)],
    )
    return pl.pallas_call(
        functools.partial(_moba_dec_kernel, per_step=per_step, pages_per_block=pages_per_block,
                          nblocks=npages // pages_per_block),
        grid_spec=grid_spec,
        out_shape=jax.ShapeDtypeStruct((b, nh, hd, 1), F32),
        compiler_params=_params("arbitrary", "arbitrary"),
        name="moba_decode",
    )(pt_flat, slopes, qcol, kn, vn, *([cache_k] * per_step), *([cache_v] * per_step))


def _split_w_in(w_in, nh_fox, fw, mw, d):
    o1 = 3 * fw
    o2 = o1 + nh_fox
    o3 = o2 + 3 * mw
    wf = w_in[:, :o1].astype(BF16)
    wl = jnp.pad(w_in[:, o1:o2], ((0, 0), (0, LANES - nh_fox))).astype(BF16)
    wm = w_in[:, o2:o3].astype(BF16)
    wg = w_in[:, o3:].astype(BF16)
    return wf, wl, wm, wg


def kernel(x_prompt, x_sample, cache_fox_k, cache_fox_v, cache_fox_logf, cache_moba_k, cache_moba_v,
           state_conv, page_table, c_prompt, c_sample, w_ada, b_ada, g_mix_pre, g_mix_post, w_in, b_forget,
           w_branch_fox, w_branch_moba, w_out, g_ffn_pre, g_ffn_post, w_up, w_conv, b_conv, w_down):
    depth = w_ada.shape[0]
    bp, s, d = x_prompt.shape
    bs, dec_s, _ = x_sample.shape
    nh_fox = b_forget.shape[1]
    fw = w_branch_fox.shape[1]
    mw = w_branch_moba.shape[1]
    nh_moba = mw // HEAD_DIM
    page_rows = cache_fox_k.shape[2]
    npages = page_table.shape[1]
    past = npages * page_rows
    tm = 512
    t_attn = 1024
    per_step = 8
    assert dec_s == 1 and fw == nh_fox * HEAD_DIM and nh_fox == SUBLANES and nh_moba == SUBLANES
    assert fw % LANES == 0 and mw % LANES == 0 and s % (SUBLANES * MOBA_BLOCK) == 0 and s % tm == 0
    assert s // MOBA_BLOCK <= SEL_LANES and s % t_attn == 0 and t_attn % MOBA_BLOCK == 0
    assert page_rows == LANES and MOBA_BLOCK % page_rows == 0 and npages % per_step == 0
    assert past % MOBA_BLOCK == 0 and MOBA_TOPK <= past // MOBA_BLOCK <= LANES
    slopes = jnp.asarray([2.0 ** (-8.0 * (h + 1) / nh_moba) for h in range(nh_moba)], F32)
    pt_flat = page_table.reshape(-1).astype(jnp.int32)
    moba_keys = _moba_key_table(s)
    fk_t, fv_t = jnp.transpose(cache_fox_k, (0, 1, 3, 4, 2)), jnp.transpose(cache_fox_v, (0, 1, 3, 4, 2))
    mk_t, mv_t = jnp.transpose(cache_moba_k, (0, 1, 3, 4, 2)), jnp.transpose(cache_moba_v, (0, 1, 3, 4, 2))
    fl_t = jnp.transpose(cache_fox_logf, (0, 1, 3, 2))

    rows_c = bp + bs
    rows_pad = -(-rows_c // SUBLANES) * SUBLANES
    c_all = jnp.pad(jnp.concatenate([c_prompt, c_sample], axis=0), ((0, rows_pad - rows_c), (0, 0)))

    y_p, y_s = x_prompt, x_sample.reshape(1, bs, d)
    outs_p, outs_s = [], []
    for l in range(depth):
        mod = _ada(c_all, w_ada[l], b_ada[l][None, :]).reshape(rows_pad, N_MOD, d)
        mod_p = [mod[:bp, i][:, None, :] for i in range(N_MOD)]
        mod_s = [mod[bp:rows_c, i][None] for i in range(N_MOD)]
        wf, wl, wm, wg = _split_w_in(w_in[l], nh_fox, fw, mw, d)
        bl = jnp.pad(b_forget[l], (0, LANES - nh_fox))[None, :]
        g1, g2 = g_mix_pre[l][None, :], g_mix_post[l][None, :]
        g3, g4 = g_ffn_pre[l][None, :], g_ffn_post[l][None, :]
        wbf, wbm, wo = w_branch_fox[l].astype(BF16), w_branch_moba[l].astype(BF16), w_out[l].astype(BF16)
        wup, wdn = w_up[l].astype(BF16), w_down[l].astype(BF16)
        wc, bc = w_conv[l], b_conv[l][None, :]

        (qf, kf, vf, kfb, vfb, lf, lft, qm, km, vm, kmb, vmb, sgf, sgm) = _inproj(
            y_p, mod_p[0], mod_p[1], g1, wf, wl, wm, wg, bl, tm=tm, nh=nh_fox)
        aq_f, ak_f = _fox_aug(_cumsum(lft), tm=tm)
        o_f = _attention(qf, aq_f, kfb, ak_f, vfb, t=t_attn, block_causal=False, name="fox_attention")
        aq_m = _moba_aug(slopes, qm, _kmean(km, LANES))
        o_m = _attention(qm, aq_m, kmb, moba_keys, vmb, t=t_attn, block_causal=True, name="moba_attention")
        x1 = _outproj(y_p, o_f, o_m, sgf, sgm, mod_p[2], g2, wbf, wbm, wo, tm=tm)
        y_p, tail = _ffn(x1, mod_p[3], mod_p[4], mod_p[5], g3, g4, wup, wc, bc, wdn, tm=tm)
        heads = lambda t, n: t.reshape(bp, s, n, HEAD_DIM)
        outs_p.append((heads(kf, nh_fox), heads(vf, nh_fox), lf, heads(km, nh_moba), heads(vm, nh_moba),
                       tail[:, SUBLANES - (CONV_W - 1):]))

        (qf, kf, vf, _, _, lf, _, qm, km, vm, _, _, sgf, sgm) = _inproj(
            y_s, mod_s[0], mod_s[1], g1, wf, wl, wm, wg, bl, tm=bs, nh=nh_fox)
        col = lambda t, n: t.astype(F32).reshape(bs, n, HEAD_DIM, 1)
        bias = _logf_bias(fl_t, l, lf.reshape(bs, nh_fox, 1), pt_flat, npages, per_step)
        o_f = _fox_decode(col(qf, nh_fox), col(kf, nh_fox), col(vf, nh_fox), bias,
                          fk_t, fv_t, l, pt_flat, npages, per_step)
        o_m = _moba_decode(col(qm, nh_moba), col(km, nh_moba), col(vm, nh_moba), slopes,
                           mk_t, mv_t, l, pt_flat, npages, per_step)
        o_f = o_f.reshape(1, bs, fw).astype(BF16)
        o_m = o_m.reshape(1, bs, mw).astype(BF16)
        x1 = _outproj(y_s, o_f, o_m, sgf, sgm, mod_s[2], g2, wbf, wbm, wo, tm=bs)
        prev = (state_conv[l][None, :, 0], state_conv[l][None, :, 1])
        y_s, u_new = _ffn(x1, mod_s[3], mod_s[4], mod_s[5], g3, g4, wup, wc, bc, wdn, prev, tm=bs)
        heads = lambda t, n: t.reshape(bs, 1, n, HEAD_DIM)
        outs_s.append((heads(kf, nh_fox), heads(vf, nh_fox), lf.reshape(bs, 1, nh_fox),
                       heads(km, nh_moba), heads(vm, nh_moba),
                       jnp.stack([state_conv[l][:, 1], u_new[0]], axis=1)))

    stack = lambda rows, i: jnp.stack([r[i] for r in rows])
    return (y_p, y_s.reshape(bs, 1, d),
            stack(outs_p, 0), stack(outs_p, 1), stack(outs_p, 2), stack(outs_p, 3), stack(outs_p, 4), stack(outs_p, 5),
            stack(outs_s, 0), stack(outs_s, 1), stack(outs_s, 2), stack(outs_s, 3), stack(outs_s, 4), stack(outs_s, 5))
```

```python
import functools

import jax
import jax.numpy as jnp
from jax import lax
from jax.experimental import pallas as pl
from jax.experimental.pallas import tpu as pltpu

F32 = jnp.float32
BF16 = jnp.bfloat16

HEAD_DIM = 64
MOBA_BLOCK = 256
MOBA_TOPK = 3
CONV_W = 3
N_MOD = 6
EPS = 1e-6
LANES = 128
SUBLANES = 8
LOG2E = 1.4426950408889634
BIG = 2.0 ** 100
VMEM_LIMIT = 56 * 1024 * 1024

SEL_LANES = 64
A_BLK = 64
A_OFF = 67
A_QPOS = 70
A_END = 73

_NT = (((1,), (1,)), ((), ()))


def _params(*sem):
    return pltpu.CompilerParams(dimension_semantics=sem, vmem_limit_bytes=VMEM_LIMIT)


def _const_spec(shape):
    nd = len(shape)
    return pl.BlockSpec(shape, lambda *_: (0,) * nd, pipeline_mode=pl.Buffered(1))


def _rms(x, g):
    return x * lax.rsqrt(jnp.mean(x * x, axis=-1, keepdims=True) + EPS) * g


def _dot(a, b):
    return jnp.dot(a, b, preferred_element_type=F32)


def _dot_nt(a, b):
    return lax.dot_general(a, b, _NT, preferred_element_type=F32)


def _split3(x):
    p1 = x.astype(BF16).astype(F32)
    r = x - p1
    p2 = r.astype(BF16).astype(F32)
    p3 = (r - p2).astype(BF16).astype(F32)
    return p1, p2, p3


def _ada_kernel(c_ref, w_ref, b_ref, o_ref):
    c = c_ref[...]
    s = (c * jax.nn.sigmoid(c)).astype(BF16)
    o_ref[...] = _dot(s, w_ref[...].astype(BF16)) + b_ref[...]


def _ada(c, w, b):
    rows, d = c.shape
    n = w.shape[1]
    tn = 1536 if n % 1536 == 0 else n
    return pl.pallas_call(
        _ada_kernel,
        grid=(n // tn,),
        in_specs=[pl.BlockSpec((rows, d), lambda j: (0, 0)),
                  pl.BlockSpec((d, tn), lambda j: (0, j)),
                  pl.BlockSpec((1, tn), lambda j: (0, j))],
        out_specs=pl.BlockSpec((rows, tn), lambda j: (0, j)),
        out_shape=jax.ShapeDtypeStruct((rows, n), F32),
        compiler_params=_params("arbitrary"),
        name="ada_mod",
    )(c, w, b)


def _inproj_kernel(x_ref, shift_ref, scale_ref, g_ref, wf_ref, wl_ref, wm_ref, wg_ref, bl_ref,
                   qf_ref, kf_ref, vf_ref, kfb_ref, vfb_ref, lf_ref, lft_ref,
                   qm_ref, km_ref, vm_ref, kmb_ref, vmb_ref, sgf_ref, sgm_ref, *, fw, mw, d, nh):
    x = x_ref[...]
    h = (_rms(x, g_ref[...]) * (1.0 + scale_ref[...]) + shift_ref[...]).astype(BF16)
    qs = LOG2E * HEAD_DIM ** -0.5

    a = _dot(h, wf_ref[...])
    qf_ref[...] = (a[:, :fw] * qs).astype(BF16)
    kf = a[:, fw:2 * fw]
    vf = a[:, 2 * fw:]
    kf_ref[...] = kf
    vf_ref[...] = vf
    kfb_ref[...] = kf.astype(BF16)
    vfb_ref[...] = vf.astype(BF16)

    z = _dot(h, wl_ref[...]) + bl_ref[...]
    lf = jnp.minimum(z, 0.0) - jnp.log1p(jnp.exp(-jnp.abs(z)))
    lf_ref[...] = lf[:, :nh]
    lft_ref[...] = lf.T[:nh, :]

    a = _dot(h, wm_ref[...])
    qm_ref[...] = (a[:, :mw] * qs).astype(BF16)
    km = a[:, mw:2 * mw]
    vm = a[:, 2 * mw:]
    km_ref[...] = km
    vm_ref[...] = vm
    kmb_ref[...] = km.astype(BF16)
    vmb_ref[...] = vm.astype(BF16)

    a = jax.nn.sigmoid(_dot(h, wg_ref[...]))
    sgf_ref[...] = a[:, :d].astype(BF16)
    sgm_ref[...] = a[:, d:].astype(BF16)


def _inproj(x, shift, scale, g, wf, wl, wm, wg, bl, *, tm, nh):
    b, s, d = x.shape
    fw = wf.shape[1] // 3
    mw = wm.shape[1] // 3
    mrows = shift.shape[1]
    if mrows == 1:
        mod_spec = pl.BlockSpec((None, 1, d), lambda bi, i: (bi, 0, 0))
    else:
        mod_spec = pl.BlockSpec((None, tm, d), lambda bi, i: (bi, i, 0))
    row = lambda w: pl.BlockSpec((None, tm, w), lambda bi, i: (bi, i, 0))
    outs = [(fw, BF16), (fw, F32), (fw, F32), (fw, BF16), (fw, BF16), (nh, F32), None,
            (mw, BF16), (mw, F32), (mw, F32), (mw, BF16), (mw, BF16), (d, BF16), (d, BF16)]
    out_shape, out_specs = [], []
    for o in outs:
        if o is None:
            out_shape.append(jax.ShapeDtypeStruct((b, nh, s), F32))
            out_specs.append(pl.BlockSpec((None, nh, tm), lambda bi, i: (bi, 0, i)))
        else:
            out_shape.append(jax.ShapeDtypeStruct((b, s, o[0]), o[1]))
            out_specs.append(row(o[0]))
    return pl.pallas_call(
        functools.partial(_inproj_kernel, fw=fw, mw=mw, d=d, nh=nh),
        grid=(b, s // tm),
        in_specs=[row(d), mod_spec, mod_spec, _const_spec((1, d)),
                  _const_spec(wf.shape), _const_spec(wl.shape), _const_spec(wm.shape),
                  _const_spec(wg.shape), _const_spec((1, LANES))],
        out_specs=out_specs,
        out_shape=out_shape,
        compiler_params=_params("arbitrary", "arbitrary"),
        name="in_proj",
    )(x, shift, scale, g, wf, wl, wm, wg, bl)


def _scan_lanes(x, reverse):
    n = x.shape[-1]
    lane = lax.broadcasted_iota(jnp.int32, x.shape, x.ndim - 1)
    sh = 1
    while sh < n:
        if reverse:
            x = x + jnp.where(lane < n - sh, pltpu.roll(x, n - sh, axis=x.ndim - 1), 0.0)
        else:
            x = x + jnp.where(lane >= sh, pltpu.roll(x, sh, axis=x.ndim - 1), 0.0)
        sh *= 2
    return x


def _cumsum_kernel(x_ref, o_ref):
    o_ref[...] = _scan_lanes(x_ref[...], reverse=False)


def _cumsum(x):
    b, nh, s = x.shape
    return pl.pallas_call(
        _cumsum_kernel,
        grid=(b,),
        in_specs=[pl.BlockSpec((None, nh, s), lambda bi: (bi, 0, 0))],
        out_specs=pl.BlockSpec((None, nh, s), lambda bi: (bi, 0, 0)),
        out_shape=jax.ShapeDtypeStruct((b, nh, s), F32),
        compiler_params=_params("arbitrary"),
        name="logf_cumsum",
    )(x)


def _fox_aug_kernel(c_ref, aq_ref, ak_ref):
    c = c_ref[...] * LOG2E
    nh, tm = c.shape
    pieces = _split3(c)
    sub = lax.broadcasted_iota(jnp.int32, (LANES, tm), 0)
    row = lambda n, h: jnp.broadcast_to(pieces[n][h:h + 1, :], (LANES, tm))
    for h in range(nh):
        o = SUBLANES * (h % 2)
        mq = jnp.where((sub >= o + 3) & (sub < o + 6), 1.0, 0.0)
        for n in range(3):
            mq = jnp.where(sub == o + n, row(n, h), mq)
        aq_ref[:, h * LANES:(h + 1) * LANES] = mq.T.astype(BF16)
    for hp in range(nh // 2):
        mk = jnp.zeros((LANES, tm), F32)
        for hh in range(2):
            o = SUBLANES * hh
            mk = jnp.where((sub >= o) & (sub < o + 3), 1.0, mk)
            for n in range(3):
                mk = jnp.where(sub == o + 3 + n, -row(n, 2 * hp + hh), mk)
        ak_ref[:, hp * LANES:(hp + 1) * LANES] = mk.T.astype(BF16)


def _fox_aug(cum, *, tm):
    b, nh, s = cum.shape
    return pl.pallas_call(
        _fox_aug_kernel,
        grid=(b, s // tm),
        in_specs=[pl.BlockSpec((None, nh, tm), lambda bi, i: (bi, 0, i))],
        out_specs=[pl.BlockSpec((None, tm, nh * LANES), lambda bi, i: (bi, i, 0)),
                   pl.BlockSpec((None, tm, nh // 2 * LANES), lambda bi, i: (bi, i, 0))],
        out_shape=[jax.ShapeDtypeStruct((b, s, nh * LANES), BF16),
                   jax.ShapeDtypeStruct((b, s, nh // 2 * LANES), BF16)],
        compiler_params=_params("arbitrary", "arbitrary"),
        name="fox_aug",
    )(cum)


def _kmean_kernel(k_ref, o_ref, *, nvalid):
    i = pl.program_id(1)

    @pl.when(i < nvalid)
    def _():
        k = k_ref[...]
        o_ref[...] = jnp.mean(k.reshape(SUBLANES, MOBA_BLOCK, k.shape[-1]), axis=1)

    @pl.when(i >= nvalid)
    def _():
        o_ref[...] = jnp.zeros_like(o_ref)


def _kmean(k, nbp):
    b, s, w = k.shape
    rows = SUBLANES * MOBA_BLOCK
    nvalid = s // rows
    return pl.pallas_call(
        functools.partial(_kmean_kernel, nvalid=nvalid),
        grid=(b, nbp // SUBLANES),
        in_specs=[pl.BlockSpec((None, rows, w), lambda bi, i: (bi, jnp.minimum(i, nvalid - 1), 0))],
        out_specs=pl.BlockSpec((None, SUBLANES, w), lambda bi, i: (bi, i, 0)),
        out_shape=jax.ShapeDtypeStruct((b, nbp, w), F32),
        compiler_params=_params("arbitrary", "arbitrary"),
        name="moba_kmean",
    )(k)


def _pick_topk(g, idx, nbp):
    sel = jnp.zeros(g.shape, jnp.bool_)
    for _ in range(MOBA_TOPK):
        mx = jnp.max(g, axis=0, keepdims=True)
        is_max = (g == mx) & (mx > -jnp.inf)
        first = jnp.min(jnp.where(is_max, idx, nbp), axis=0, keepdims=True)
        pick = idx == first
        sel = sel | pick
        g = jnp.where(pick, -jnp.inf, g)
    return sel


def _moba_aug_kernel(slope_ref, q_ref, km_ref, aq_ref):
    t = MOBA_BLOCK
    hp = pl.program_id(1)
    i = pl.program_id(2)
    q2 = q_ref[...]
    km = km_ref[...]
    km_hi = km.astype(BF16)
    km_lo = (km - km_hi.astype(F32)).astype(BF16)
    lane = lax.broadcasted_iota(jnp.int32, (1, LANES), 1)
    blk = lax.broadcasted_iota(jnp.int32, (t, LANES), 1)
    blk_t = lax.broadcasted_iota(jnp.int32, (LANES, t), 0)
    qpos = (i * t + lax.broadcasted_iota(jnp.int32, (t, LANES), 0)).astype(F32)
    zero = jnp.zeros_like(q2)
    for hh in range(2):
        sig = jnp.full((t, LANES), slope_ref[2 * hp + hh] * LOG2E, F32)
        qh = jnp.where((lane < HEAD_DIM) if hh == 0 else (lane >= HEAD_DIM), q2, zero)
        gate = _dot_nt(km_hi, qh) + _dot_nt(km_lo, qh)
        sel = _pick_topk(jnp.where(blk_t < i, gate, -jnp.inf), blk_t, LANES)
        aug = jnp.where(sel | (blk_t == i), 0.0, -BIG).T
        s3 = _split3(sig)
        t3 = _split3(-sig * qpos)
        for n in range(3):
            aug = jnp.where((blk == A_BLK + n) | (blk == A_OFF + n), s3[n], aug)
            aug = jnp.where(blk == A_QPOS + n, t3[n], aug)
        aug = jnp.where(blk >= A_END, 0.0, aug)
        aq_ref[:, hh * LANES:(hh + 1) * LANES] = aug.astype(BF16)


def _moba_aug(slopes, q, kmean):
    b, s, w = q.shape
    npair = w // LANES
    t = MOBA_BLOCK
    return pl.pallas_call(
        _moba_aug_kernel,
        grid=(b, npair, s // t),
        in_specs=[pl.BlockSpec(memory_space=pltpu.SMEM),
                  pl.BlockSpec((None, t, LANES), lambda bi, hp, i: (bi, i, hp)),
                  pl.BlockSpec((None, LANES, LANES), lambda bi, hp, i: (bi, 0, hp))],
        out_specs=pl.BlockSpec((None, t, 2 * LANES), lambda bi, hp, i: (bi, i, hp)),
        out_shape=jax.ShapeDtypeStruct((b, s, 2 * w), BF16),
        compiler_params=_params("arbitrary", "arbitrary", "arbitrary"),
        name="moba_aug",
    )(slopes, q, kmean)


def _moba_key_table(s):
    pos = jnp.arange(s, dtype=jnp.int32)[:, None]
    lane = jnp.arange(LANES, dtype=jnp.int32)[None, :]
    blk = pos // MOBA_BLOCK
    tab = jnp.where(lane == blk, 1, 0)
    tab = jnp.where((lane >= A_BLK) & (lane < A_OFF), MOBA_BLOCK * blk, tab)
    tab = jnp.where((lane >= A_OFF) & (lane < A_QPOS), pos % MOBA_BLOCK, tab)
    tab = jnp.where((lane >= A_QPOS) & (lane < A_END), 1, tab)
    return tab.astype(BF16)[None]


def _attn_kernel(q_ref, aq_ref, k_ref, ak_ref, v_ref, o_ref, *, t, block_causal):
    i = pl.program_id(2)
    q2 = q_ref[...]
    lane = lax.broadcasted_iota(jnp.int32, (1, LANES), 1)
    zero = jnp.zeros_like(q2)
    qa = (jnp.concatenate([jnp.where(lane < HEAD_DIM, q2, zero), aq_ref[:, :LANES]], axis=1),
          jnp.concatenate([jnp.where(lane >= HEAD_DIM, q2, zero), aq_ref[:, LANES:]], axis=1))
    ones = jnp.ones((t, LANES), BF16)

    def step(s, va, m, acc):
        part = s[:, :LANES]
        for c in range(1, t // LANES):
            part = jnp.maximum(part, s[:, c * LANES:(c + 1) * LANES])
        m_new = jnp.maximum(m, jnp.max(part, axis=-1, keepdims=True))
        p = jnp.exp2(s - m_new)
        acc = jnp.exp2(m - m_new) * acc + _dot(p.astype(BF16), va)
        return m_new, acc

    def tiles(j):
        ks = pl.multiple_of(j * t, t)
        ka = jnp.concatenate([k_ref[pl.ds(ks, t), :], ak_ref[pl.ds(ks, t), :]], axis=1)
        v = v_ref[pl.ds(ks, t), :]
        return ka, (jnp.where(lane < HEAD_DIM, v, ones), jnp.where(lane >= HEAD_DIM, v, ones))

    def body(j, carry):
        ka, va = tiles(j)
        return tuple(step(_dot_nt(qa[hh], ka), va[hh], *carry[hh]) for hh in range(2))

    init = (jnp.full((t, 1), -jnp.inf, F32), jnp.zeros((t, LANES), F32))
    carry = lax.fori_loop(0, i, body, (init, init))

    ka, va = tiles(i)
    row = lax.broadcasted_iota(jnp.int32, (t, t), 0)
    col = lax.broadcasted_iota(jnp.int32, (t, t), 1)
    future = col > row
    if block_causal:
        future = future & ((col // MOBA_BLOCK) == (row // MOBA_BLOCK))
    outs = []
    for hh in range(2):
        s = jnp.where(future, -jnp.inf, _dot_nt(qa[hh], ka))
        _, acc = step(s, va[hh], *carry[hh])
        outs.append(acc / pltpu.roll(acc, HEAD_DIM, axis=1))
    o_ref[...] = jnp.where(lane < HEAD_DIM, outs[0], outs[1]).astype(BF16)


def _attention(q, aq, k, ak, v, *, t, block_causal, name):
    b, s, w = q.shape
    npair = w // LANES
    per_seq = ak.shape[0] > 1
    per_pair = ak.shape[2] > LANES
    return pl.pallas_call(
        functools.partial(_attn_kernel, t=t, block_causal=block_causal),
        grid=(b, npair, s // t),
        in_specs=[pl.BlockSpec((None, t, LANES), lambda bi, hp, i: (bi, i, hp)),
                  pl.BlockSpec((None, t, 2 * LANES), lambda bi, hp, i: (bi, i, hp)),
                  pl.BlockSpec((None, s, LANES), lambda bi, hp, i: (bi, 0, hp)),
                  pl.BlockSpec((None, s, LANES),
                               lambda bi, hp, i: (bi if per_seq else 0, 0, hp if per_pair else 0)),
                  pl.BlockSpec((None, s, LANES), lambda bi, hp, i: (bi, 0, hp))],
        out_specs=pl.BlockSpec((None, t, LANES), lambda bi, hp, i: (bi, i, hp)),
        out_shape=jax.ShapeDtypeStruct((b, s, w), BF16),
        compiler_params=_params("arbitrary", "arbitrary", "arbitrary"),
        name=name,
    )(q, aq, k, ak, v)


def _outproj_kernel(x_ref, of_ref, om_ref, sgf_ref, sgm_ref, gate_ref, g_ref, wbf_ref, wbm_ref, wo_ref, y_ref):
    yf = _dot(of_ref[...], wbf_ref[...])
    ym = _dot(om_ref[...], wbm_ref[...])
    mixed = sgf_ref[...].astype(F32) * yf + sgm_ref[...].astype(F32) * ym
    z = _dot(mixed.astype(BF16), wo_ref[...])
    y_ref[...] = x_ref[...] + gate_ref[...] * _rms(z, g_ref[...])


def _outproj(x, of, om, sgf, sgm, gate, g, wbf, wbm, wo, *, tm):
    b, s, d = x.shape
    if gate.shape[1] == 1:
        mod_spec = pl.BlockSpec((None, 1, d), lambda bi, i: (bi, 0, 0))
    else:
        mod_spec = pl.BlockSpec((None, tm, d), lambda bi, i: (bi, i, 0))
    row = lambda w: pl.BlockSpec((None, tm, w), lambda bi, i: (bi, i, 0))
    return pl.pallas_call(
        _outproj_kernel,
        grid=(b, s // tm),
        in_specs=[row(d), row(of.shape[-1]), row(om.shape[-1]), row(d), row(d), mod_spec,
                  _const_spec((1, d)), _const_spec(wbf.shape), _const_spec(wbm.shape), _const_spec(wo.shape)],
        out_specs=row(d),
        out_shape=jax.ShapeDtypeStruct((b, s, d), F32),
        compiler_params=_params("arbitrary", "arbitrary"),
        name="out_proj",
    )(x, of, om, sgf, sgm, gate, g, wbf, wbm, wo)


def _gelu_tanh(x):
    c = 0.7978845608028654
    return 0.5 * x * (1.0 + jnp.tanh(c * (x + 0.044715 * (x * x * x))))


def _ffn_kernel(*refs, dff, ch, tm, stepwise):
    if stepwise:
        (x_ref, shift_ref, scale_ref, gate_ref, gpre_ref, gpost_ref, wup_ref, wc_ref, bc_ref, wdn_ref,
         p2_ref, p1_ref, y_ref, u_ref, acc_ref) = refs
    else:
        (x_ref, shift_ref, scale_ref, gate_ref, gpre_ref, gpost_ref, wup_ref, wc_ref, bc_ref, wdn_ref,
         y_ref, tail_ref, acc_ref, ubuf_ref, carry_ref) = refs
        first = pl.program_id(1) == 0
    x = x_ref[...]
    h = (_rms(x, gpre_ref[...]) * (1.0 + scale_ref[...]) + shift_ref[...]).astype(BF16)
    acc_ref[...] = jnp.zeros_like(acc_ref)
    for c in range(dff // ch):
        cs = slice(c * ch, (c + 1) * ch)
        u = _dot(h, wup_ref[:, cs])
        v = _dot(h, wup_ref[:, dff + c * ch:dff + (c + 1) * ch])
        if stepwise:
            u2 = p2_ref[:, cs]
            u1 = p1_ref[:, cs]
            u_ref[:, cs] = u
        else:
            @pl.when(first)
            def _():
                carry_ref[c] = jnp.zeros((SUBLANES, ch), F32)

            ubuf_ref[0:SUBLANES, :] = carry_ref[c]
            ubuf_ref[SUBLANES:SUBLANES + tm, :] = u
            u1 = ubuf_ref[SUBLANES - 1:SUBLANES - 1 + tm, :]
            u2 = ubuf_ref[SUBLANES - 2:SUBLANES - 2 + tm, :]
            tail = u[tm - SUBLANES:, :]
            carry_ref[c] = tail
            tail_ref[:, cs] = tail
        uc = bc_ref[:, cs] + (wc_ref[0:1, cs] * u2 + wc_ref[1:2, cs] * u1 + wc_ref[2:3, cs] * u)
        g = (_gelu_tanh(uc) * v).astype(BF16)
        acc_ref[...] += _dot(g, wdn_ref[cs, :])
    y_ref[...] = x + gate_ref[...] * _rms(acc_ref[...], gpost_ref[...])


def _ffn(x, shift, scale, gate, gpre, gpost, wup, wc, bc, wdn, prev=None, *, tm):
    b, s, d = x.shape
    dff = wdn.shape[0]
    ch = 256 if dff % 256 == 0 else LANES
    stepwise = prev is not None
    if shift.shape[1] == 1:
        mod_spec = pl.BlockSpec((None, 1, d), lambda bi, i: (bi, 0, 0))
    else:
        mod_spec = pl.BlockSpec((None, tm, d), lambda bi, i: (bi, i, 0))
    row = lambda w: pl.BlockSpec((None, tm, w), lambda bi, i: (bi, i, 0))
    in_specs = [row(d), mod_spec, mod_spec, mod_spec, _const_spec((1, d)), _const_spec((1, d)),
                _const_spec(wup.shape), _const_spec(wc.shape), _const_spec(bc.shape), _const_spec(wdn.shape)]
    args = [x, shift, scale, gate, gpre, gpost, wup, wc, bc, wdn]
    scratch = [pltpu.VMEM((tm, d), F32)]
    if stepwise:
        in_specs += [row(dff), row(dff)]
        args += list(prev)
        out_specs = [row(d), row(dff)]
        out_shape = [jax.ShapeDtypeStruct((b, s, d), F32), jax.ShapeDtypeStruct((b, s, dff), F32)]
    else:
        out_specs = [row(d), pl.BlockSpec((None, SUBLANES, dff), lambda bi, i: (bi, 0, 0))]
        out_shape = [jax.ShapeDtypeStruct((b, s, d), F32), jax.ShapeDtypeStruct((b, SUBLANES, dff), F32)]
        scratch += [pltpu.VMEM((tm + SUBLANES, ch), F32), pltpu.VMEM((dff // ch, SUBLANES, ch), F32)]
    return pl.pallas_call(
        functools.partial(_ffn_kernel, dff=dff, ch=ch, tm=tm, stepwise=stepwise),
        grid=(b, s // tm),
        in_specs=in_specs,
        out_specs=out_specs,
        out_shape=out_shape,
        scratch_shapes=scratch,
        compiler_params=_params("arbitrary", "arbitrary"),
        name="ffn_step" if stepwise else "ffn_seq",
    )(*args)


def _page_spec(block, layer, npages_per_seq, per_step, p, seq_of=lambda bi: bi):
    def index(bi, g, pt_ref):
        page = pt_ref[seq_of(bi) * npages_per_seq + g * per_step + p]
        return (layer, page) + (0,) * (len(block) - 2)
    return pl.BlockSpec(block, index)


def _col_spec(nh, hd, seq_of=lambda bi: bi):
    return pl.BlockSpec((None, nh, hd, 1), lambda bi, g, pt: (seq_of(bi), 0, 0, 0))


def _stack_rows(rows):
    shape = (len(rows), rows[0].shape[1])
    sub = lax.broadcasted_iota(jnp.int32, shape, 0)
    out = jnp.broadcast_to(rows[0], shape)
    for h in range(1, len(rows)):
        out = jnp.where(sub == h, jnp.broadcast_to(rows[h], shape), out)
    return out


def _page_scores(k_ref, qb_ref):
    nh = k_ref.shape[0]
    return _stack_rows([jnp.sum(k_ref[h] * qb_ref[h], axis=0, keepdims=True) for h in range(nh)])


def _logf_bias_kernel(pt_ref, new_ref, *refs, per_step):
    pages = refs[:per_step]
    o_ref, x_ref = refs[per_step:]
    g = pl.program_id(1)
    rows = pages[0].shape[-1]
    for p in range(per_step):
        x_ref[:, pl.ds(pl.multiple_of((g * per_step + p) * rows, rows), rows)] = pages[p][...]

    @pl.when(g == pl.num_programs(1) - 1)
    def _():
        x = x_ref[...]
        n = x.shape[-1]
        lane = lax.broadcasted_iota(jnp.int32, x.shape, 1)
        shifted = jnp.where(lane < n - 1, pltpu.roll(x, n - 1, axis=1), 0.0)
        o_ref[...] = (_scan_lanes(shifted, reverse=True) + new_ref[...]) * LOG2E


def _logf_bias(cache_t, layer, new, pt_flat, npages, per_step):
    _, _, nh, rows = cache_t.shape
    b = new.shape[0]
    n = npages * rows
    grid_spec = pltpu.PrefetchScalarGridSpec(
        num_scalar_prefetch=1,
        grid=(b, npages // per_step),
        in_specs=[pl.BlockSpec((None, nh, 1), lambda bi, g, pt: (bi, 0, 0))]
                 + [_page_spec((None, None, nh, rows), layer, npages, per_step, p) for p in range(per_step)],
        out_specs=pl.BlockSpec((None, nh, n), lambda bi, g, pt: (bi, 0, 0)),
        scratch_shapes=[pltpu.VMEM((nh, n), F32)],
    )
    return pl.pallas_call(
        functools.partial(_logf_bias_kernel, per_step=per_step),
        grid_spec=grid_spec,
        out_shape=jax.ShapeDtypeStruct((b, nh, n), F32),
        compiler_params=_params("arbitrary", "arbitrary"),
        name="logf_bias",
    )(pt_flat, new, *([cache_t] * per_step))


def _fox_dec_kernel(pt_ref, q_ref, kn_ref, vn_ref, d_ref, *refs, per_step):
    kp = refs[:per_step]
    vp = refs[per_step:2 * per_step]
    o_ref, qb_ref, m_ref, l_ref, acc_ref = refs[2 * per_step:]
    g = pl.program_id(1)
    nh, hd, rows = kp[0].shape

    @pl.when(g == 0)
    def _():
        for h in range(nh):
            qb_ref[h] = jnp.broadcast_to(q_ref[h], (hd, rows))
        m_ref[...] = jnp.full_like(m_ref, -BIG)
        l_ref[...] = jnp.zeros_like(l_ref)
        acc_ref[...] = jnp.zeros_like(acc_ref)

    ss = []
    m_new = m_ref[...]
    for p in range(per_step):
        s = _page_scores(kp[p], qb_ref) + d_ref[:, p * rows:(p + 1) * rows]
        ss.append(s)
        m_new = jnp.maximum(m_new, jnp.max(s, axis=-1, keepdims=True))
    alpha = jnp.exp2(m_ref[...] - m_new)
    l = alpha * l_ref[...]
    prs = []
    for p in range(per_step):
        pr = jnp.exp2(ss[p] - m_new)
        prs.append(pr)
        l = l + jnp.sum(pr, axis=-1, keepdims=True)
    for h in range(nh):
        acc = alpha[h:h + 1, :] * acc_ref[h]
        for p in range(per_step):
            acc = acc + vp[p][h] * prs[p][h:h + 1, :]
        acc_ref[h] = acc
    m_ref[...] = m_new
    l_ref[...] = l

    @pl.when(g == pl.num_programs(1) - 1)
    def _():
        for h in range(nh):
            s_self = jnp.sum(q_ref[h] * kn_ref[h], axis=0, keepdims=True)
            m_h = m_new[h:h + 1, :]
            m_fin = jnp.maximum(m_h, s_self)
            a = jnp.exp2(m_h - m_fin)
            p_self = jnp.exp2(s_self - m_fin)
            num = a * jnp.sum(acc_ref[h], axis=-1, keepdims=True) + p_self * vn_ref[h]
            o_ref[h] = num / (a * l[h:h + 1, :] + p_self)


def _fox_decode(qcol, kn, vn, bias, cache_k, cache_v, layer, pt_flat, npages, per_step):
    b = qcol.shape[0]
    _, _, nh, hd, rows = cache_k.shape
    page = lambda p: _page_spec((None, None, nh, hd, rows), layer, npages, per_step, p)
    grid_spec = pltpu.PrefetchScalarGridSpec(
        num_scalar_prefetch=1,
        grid=(b, npages // per_step),
        in_specs=[_col_spec(nh, hd), _col_spec(nh, hd), _col_spec(nh, hd),
                  pl.BlockSpec((None, nh, per_step * rows), lambda bi, g, pt: (bi, 0, g))]
                 + [page(p) for p in range(per_step)] + [page(p) for p in range(per_step)],
        out_specs=_col_spec(nh, hd),
        scratch_shapes=[pltpu.VMEM((nh, hd, rows), F32), pltpu.VMEM((nh, 1), F32), pltpu.VMEM((nh, 1), F32),
                        pltpu.VMEM((nh, hd, rows), F32)],
    )
    return pl.pallas_call(
        functools.partial(_fox_dec_kernel, per_step=per_step),
        grid_spec=grid_spec,
        out_shape=jax.ShapeDtypeStruct((b, nh, hd, 1), F32),
        compiler_params=_params("arbitrary", "arbitrary"),
        name="fox_decode",
    )(pt_flat, qcol, kn, vn, bias, *([cache_k] * per_step), *([cache_v] * per_step))


def _moba_dec_kernel(pt_ref, slope_ref, q_ref, kn_ref, vn_ref, *refs, per_step, pages_per_block, nblocks):
    kp = refs[:per_step]
    vp = refs[per_step:2 * per_step]
    o_ref, qb_ref, s_ref, gate_ref, w_ref, l_ref, acc_ref = refs[2 * per_step:]
    bb = pl.program_id(0)
    g = pl.program_id(1)
    nseq = pl.num_programs(0) - 1
    ng = pl.num_programs(1)
    nh, hd, rows = kp[0].shape
    npages = nblocks * pages_per_block
    past = npages * rows
    lane = lax.broadcasted_iota(jnp.int32, (nh, LANES), 1)
    slot = bb % 2
    prev = 1 - slot
    keys_on = bb < nseq
    vals_on = bb >= 1

    @pl.when(keys_on & (g == 0))
    def _():
        for h in range(nh):
            qb_ref[h] = jnp.broadcast_to(q_ref[h], (hd, rows))
        gate_ref[...] = jnp.zeros_like(gate_ref)

    @pl.when(keys_on)
    def _():
        gate = gate_ref[...]
        for p in range(per_step):
            page = g * per_step + p
            s = _page_scores(kp[p], qb_ref)
            s_ref[slot, :, pl.ds(pl.multiple_of(page * rows, rows), rows)] = s
            gate = gate + jnp.where(lane == page // pages_per_block, jnp.sum(s, axis=-1, keepdims=True), 0.0)
        gate_ref[...] = gate

    @pl.when(keys_on & (g == ng - 1))
    def _():
        gsel = jnp.where(lane < nblocks, gate_ref[...], -jnp.inf)
        chosen = jnp.zeros((nh, LANES), F32)
        for _ in range(MOBA_TOPK):
            mx = jnp.max(gsel, axis=-1, keepdims=True)
            first = jnp.min(jnp.where(gsel == mx, lane, LANES), axis=-1, keepdims=True)
            chosen = jnp.where(lane == first, 1.0, chosen)
            gsel = jnp.where(lane == first, -jnp.inf, gsel)
        sub = lax.broadcasted_iota(jnp.int32, (nh, 1), 0)
        sig = jnp.zeros((nh, 1), F32)
        for h in range(nh):
            sig = jnp.where(sub == h, slope_ref[h] * LOG2E, sig)
        picked = jnp.concatenate(
            [jnp.broadcast_to(jnp.sum(jnp.where(lane == n, chosen, 0.0), axis=-1, keepdims=True),
                              (nh, MOBA_BLOCK)) for n in range(nblocks)], axis=1)
        pos = lax.broadcasted_iota(jnp.int32, (nh, past), 1)
        sb = jnp.where(picked > 0.0, s_ref[slot] - sig * (past - pos).astype(F32), -BIG)
        s_self = _stack_rows([jnp.sum(q_ref[h] * kn_ref[h], axis=0, keepdims=True) for h in range(nh)])
        m = jnp.maximum(jnp.max(sb, axis=-1, keepdims=True), s_self)
        pr = jnp.exp2(sb - m)
        w_self = jnp.exp2(s_self - m)
        s_ref[slot] = pr
        w_ref[slot] = w_self
        l_ref[slot] = jnp.sum(pr, axis=-1, keepdims=True) + w_self

    @pl.when(vals_on & (g == 0))
    def _():
        acc_ref[...] = jnp.zeros_like(acc_ref)

    @pl.when(vals_on)
    def _():
        for h in range(nh):
            acc = acc_ref[h]
            for p in range(per_step):
                page = g * per_step + p
                pr = s_ref[prev, pl.ds(h, 1), pl.ds(pl.multiple_of(page * rows, rows), rows)]
                acc = acc + vp[p][h] * pr
            acc_ref[h] = acc

    @pl.when(vals_on & (g == ng - 1))
    def _():
        w_self = w_ref[prev]
        l = l_ref[prev]
        for h in range(nh):
            num = jnp.sum(acc_ref[h], axis=-1, keepdims=True) + w_self[h:h + 1, :] * vn_ref[h]
            o_ref[h] = num / l[h:h + 1, :]


def _moba_decode(qcol, kn, vn, slopes, cache_k, cache_v, layer, pt_flat, npages, per_step):
    b = qcol.shape[0]
    _, _, nh, hd, rows = cache_k.shape
    ng = npages // per_step
    pages_per_block = MOBA_BLOCK // rows
    kseq = lambda bi: jnp.minimum(bi, b - 1)
    vseq = lambda bi: jnp.maximum(bi - 1, 0)
    kpage = lambda p: _page_spec((None, None, nh, hd, rows), layer, npages, per_step, p, seq_of=kseq)
    vpage = lambda p: _page_spec((None, None, nh, hd, rows), layer, npages, per_step, p, seq_of=vseq)
    grid_spec = pltpu.PrefetchScalarGridSpec(
        num_scalar_prefetch=1,
        grid=(b + 1, ng),
        in_specs=[pl.BlockSpec(memory_space=pltpu.SMEM), _col_spec(nh, hd, kseq), _col_spec(nh, hd, kseq),
                  _col_spec(nh, hd, vseq)]
                 + [kpage(p) for p in range(per_step)] + [vpage(p) for p in range(per_step)],
        out_specs=_col_spec(nh, hd, vseq),
        scratch_shapes=[pltpu.VMEM((nh, hd, rows), F32), pltpu.VMEM((2, nh, npages * rows), F32),
                        pltpu.VMEM((nh, LANES), F32), pltpu.VMEM((2, nh, 1), F32), pltpu.VMEM((2, nh, 1), F32),
                        pltpu.VMEM((nh, hd, rows), F32)],
    )
    return pl.pallas_call(
        functools.partial(_moba_dec_kernel, per_step=per_step, pages_per_block=pages_per_block,
                          nblocks=npages // pages_per_block),
        grid_spec=grid_spec,
        out_shape=jax.ShapeDtypeStruct((b, nh, hd, 1), F32),
        compiler_params=_params("arbitrary", "arbitrary"),
        name="moba_decode",
    )(pt_flat, slopes, qcol, kn, vn, *([cache_k] * per_step), *([cache_v] * per_step))


def _split_w_in(w_in, nh_fox, fw, mw, d):
    o1 = 3 * fw
    o2 = o1 + nh_fox
    o3 = o2 + 3 * mw
    wf = w_in[:, :o1].astype(BF16)
    wl = jnp.pad(w_in[:, o1:o2], ((0, 0), (0, LANES - nh_fox))).astype(BF16)
    wm = w_in[:, o2:o3].astype(BF16)
    wg = w_in[:, o3:].astype(BF16)
    return wf, wl, wm, wg


def kernel(x_prompt, x_sample, cache_fox_k, cache_fox_v, cache_fox_logf, cache_moba_k, cache_moba_v,
           state_conv, page_table, c_prompt, c_sample, w_ada, b_ada, g_mix_pre, g_mix_post, w_in, b_forget,
           w_branch_fox, w_branch_moba, w_out, g_ffn_pre, g_ffn_post, w_up, w_conv, b_conv, w_down):
    depth = w_ada.shape[0]
    bp, s, d = x_prompt.shape
    bs, dec_s, _ = x_sample.shape
    nh_fox = b_forget.shape[1]
    fw = w_branch_fox.shape[1]
    mw = w_branch_moba.shape[1]
    nh_moba = mw // HEAD_DIM
    page_rows = cache_fox_k.shape[2]
    npages = page_table.shape[1]
    past = npages * page_rows
    tm = 512
    t_attn = 1024
    per_step = min(16, npages)
    assert dec_s == 1 and fw == nh_fox * HEAD_DIM and nh_fox == SUBLANES and nh_moba == SUBLANES
    assert fw % LANES == 0 and mw % LANES == 0 and s % (SUBLANES * MOBA_BLOCK) == 0 and s % tm == 0
    assert s // MOBA_BLOCK <= SEL_LANES and s % t_attn == 0 and t_attn % MOBA_BLOCK == 0
    assert page_rows == LANES and MOBA_BLOCK % page_rows == 0 and npages % per_step == 0
    assert past % MOBA_BLOCK == 0 and MOBA_TOPK <= past // MOBA_BLOCK <= LANES
    slopes = jnp.asarray([2.0 ** (-8.0 * (h + 1) / nh_moba) for h in range(nh_moba)], F32)
    pt_flat = page_table.reshape(-1).astype(jnp.int32)
    moba_keys = _moba_key_table(s)
    fk_t, fv_t = jnp.transpose(cache_fox_k, (0, 1, 3, 4, 2)), jnp.transpose(cache_fox_v, (0, 1, 3, 4, 2))
    mk_t, mv_t = jnp.transpose(cache_moba_k, (0, 1, 3, 4, 2)), jnp.transpose(cache_moba_v, (0, 1, 3, 4, 2))
    fl_t = jnp.transpose(cache_fox_logf, (0, 1, 3, 2))

    rows_c = bp + bs
    rows_pad = -(-rows_c // SUBLANES) * SUBLANES
    c_all = jnp.pad(jnp.concatenate([c_prompt, c_sample], axis=0), ((0, rows_pad - rows_c), (0, 0)))

    y_p, y_s = x_prompt, x_sample.reshape(1, bs, d)
    outs_p, outs_s = [], []
    for l in range(depth):
        mod = _ada(c_all, w_ada[l], b_ada[l][None, :]).reshape(rows_pad, N_MOD, d)
        mod_p = [mod[:bp, i][:, None, :] for i in range(N_MOD)]
        mod_s = [mod[bp:rows_c, i][None] for i in range(N_MOD)]
        wf, wl, wm, wg = _split_w_in(w_in[l], nh_fox, fw, mw, d)
        bl = jnp.pad(b_forget[l], (0, LANES - nh_fox))[None, :]
        g1, g2 = g_mix_pre[l][None, :], g_mix_post[l][None, :]
        g3, g4 = g_ffn_pre[l][None, :], g_ffn_post[l][None, :]
        wbf, wbm, wo = w_branch_fox[l].astype(BF16), w_branch_moba[l].astype(BF16), w_out[l].astype(BF16)
        wup, wdn = w_up[l].astype(BF16), w_down[l].astype(BF16)
        wc, bc = w_conv[l], b_conv[l][None, :]

        (qf, kf, vf, kfb, vfb, lf, lft, qm, km, vm, kmb, vmb, sgf, sgm) = _inproj(
            y_p, mod_p[0], mod_p[1], g1, wf, wl, wm, wg, bl, tm=tm, nh=nh_fox)
        aq_f, ak_f = _fox_aug(_cumsum(lft), tm=tm)
        o_f = _attention(qf, aq_f, kfb, ak_f, vfb, t=t_attn, block_causal=False, name="fox_attention")
        aq_m = _moba_aug(slopes, qm, _kmean(km, LANES))
        o_m = _attention(qm, aq_m, kmb, moba_keys, vmb, t=t_attn, block_causal=True, name="moba_attention")
        x1 = _outproj(y_p, o_f, o_m, sgf, sgm, mod_p[2], g2, wbf, wbm, wo, tm=tm)
        y_p, tail = _ffn(x1, mod_p[3], mod_p[4], mod_p[5], g3, g4, wup, wc, bc, wdn, tm=tm)
        heads = lambda t, n: t.reshape(bp, s, n, HEAD_DIM)
        outs_p.append((heads(kf, nh_fox), heads(vf, nh_fox), lf, heads(km, nh_moba), heads(vm, nh_moba),
                       tail[:, SUBLANES - (CONV_W - 1):]))

        (qf, kf, vf, _, _, lf, _, qm, km, vm, _, _, sgf, sgm) = _inproj(
            y_s, mod_s[0], mod_s[1], g1, wf, wl, wm, wg, bl, tm=bs, nh=nh_fox)
        col = lambda t, n: t.astype(F32).reshape(bs, n, HEAD_DIM, 1)
        bias = _logf_bias(fl_t, l, lf.reshape(bs, nh_fox, 1), pt_flat, npages, npages)
        o_f = _fox_decode(col(qf, nh_fox), col(kf, nh_fox), col(vf, nh_fox), bias,
                          fk_t, fv_t, l, pt_flat, npages, per_step)
        o_m = _moba_decode(col(qm, nh_moba), col(km, nh_moba), col(vm, nh_moba), slopes,
                           mk_t, mv_t, l, pt_flat, npages, per_step)
        o_f = o_f.reshape(1, bs, fw).astype(BF16)
        o_m = o_m.reshape(1, bs, mw).astype(BF16)
        x1 = _outproj(y_s, o_f, o_m, sgf, sgm, mod_s[2], g2, wbf, wbm, wo, tm=bs)
        prev = (state_conv[l][None, :, 0], state_conv[l][None, :, 1])
        y_s, u_new = _ffn(x1, mod_s[3], mod_s[4], mod_s[5], g3, g4, wup, wc, bc, wdn, prev, tm=bs)
        heads = lambda t, n: t.reshape(bs, 1, n, HEAD_DIM)
        outs_s.append((heads(kf, nh_fox), heads(vf, nh_fox), lf.reshape(bs, 1, nh_fox),
                       heads(km, nh_moba), heads(vm, nh_moba),
                       jnp.stack([state_conv[l][:, 1], u_new[0]], axis=1)))

    stack = lambda rows, i: jnp.stack([r[i] for r in rows])
    return (y_p, y_s.reshape(bs, 1, d),
            stack(outs_p, 0), stack(outs_p, 1), stack(outs_p, 2), stack(outs_p, 3), stack(outs_p, 4), stack(outs_p, 5),
            stack(outs_s, 0), stack(outs_s, 1), stack(outs_s, 2), stack(outs_s, 3), stack(outs_s, 4), stack(outs_s, 5))
```

```python
import functools

import jax
import jax.numpy as jnp
from jax import lax
from jax.experimental import pallas as pl
from jax.experimental.pallas import tpu as pltpu

F32 = jnp.float32
BF16 = jnp.bfloat16

HEAD_DIM = 64
MOBA_BLOCK = 256
MOBA_TOPK = 3
CONV_W = 3
N_MOD = 6
EPS = 1e-6
LANES = 128
SUBLANES = 8
LOG2E = 1.4426950408889634
BIG = 2.0 ** 100
VMEM_LIMIT = 56 * 1024 * 1024

SEL_LANES = 64
A_BLK = 64
A_OFF = 67
A_QPOS = 70
A_END = 73

_NT = (((1,), (1,)), ((), ()))


def _params(*sem):
    return pltpu.CompilerParams(dimension_semantics=sem, vmem_limit_bytes=VMEM_LIMIT)


def _const_spec(shape):
    nd = len(shape)
    return pl.BlockSpec(shape, lambda *_: (0,) * nd, pipeline_mode=pl.Buffered(1))


def _rms(x, g):
    return x * lax.rsqrt(jnp.mean(x * x, axis=-1, keepdims=True) + EPS) * g


def _dot(a, b):
    return jnp.dot(a, b, preferred_element_type=F32)


def _dot_nt(a, b):
    return lax.dot_general(a, b, _NT, preferred_element_type=F32)


def _split3(x):
    p1 = x.astype(BF16).astype(F32)
    r = x - p1
    p2 = r.astype(BF16).astype(F32)
    p3 = (r - p2).astype(BF16).astype(F32)
    return p1, p2, p3


def _ada_kernel(c_ref, w_ref, b_ref, o_ref):
    c = c_ref[...]
    s = (c * jax.nn.sigmoid(c)).astype(BF16)
    o_ref[...] = _dot(s, w_ref[...].astype(BF16)) + b_ref[...]


def _ada(c, w, b):
    rows, d = c.shape
    n = w.shape[1]
    tn = 1536 if n % 1536 == 0 else n
    return pl.pallas_call(
        _ada_kernel,
        grid=(n // tn,),
        in_specs=[pl.BlockSpec((rows, d), lambda j: (0, 0)),
                  pl.BlockSpec((d, tn), lambda j: (0, j)),
                  pl.BlockSpec((1, tn), lambda j: (0, j))],
        out_specs=pl.BlockSpec((rows, tn), lambda j: (0, j)),
        out_shape=jax.ShapeDtypeStruct((rows, n), F32),
        compiler_params=_params("arbitrary"),
        name="ada_mod",
    )(c, w, b)


def _inproj_kernel(x_ref, shift_ref, scale_ref, g_ref, wf_ref, wl_ref, wm_ref, wg_ref, bl_ref,
                   qf_ref, kf_ref, vf_ref, kfb_ref, vfb_ref, lf_ref, lft_ref,
                   qm_ref, km_ref, vm_ref, kmb_ref, vmb_ref, sgf_ref, sgm_ref, *, fw, mw, d, nh):
    x = x_ref[...]
    h = (_rms(x, g_ref[...]) * (1.0 + scale_ref[...]) + shift_ref[...]).astype(BF16)
    qs = LOG2E * HEAD_DIM ** -0.5

    a = _dot(h, wf_ref[...])
    qf_ref[...] = (a[:, :fw] * qs).astype(BF16)
    kf = a[:, fw:2 * fw]
    vf = a[:, 2 * fw:]
    kf_ref[...] = kf
    vf_ref[...] = vf
    kfb_ref[...] = kf.astype(BF16)
    vfb_ref[...] = vf.astype(BF16)

    z = _dot(h, wl_ref[...]) + bl_ref[...]
    lf = jnp.minimum(z, 0.0) - jnp.log1p(jnp.exp(-jnp.abs(z)))
    lf_ref[...] = lf[:, :nh]
    lft_ref[...] = lf.T[:nh, :]

    a = _dot(h, wm_ref[...])
    qm_ref[...] = (a[:, :mw] * qs).astype(BF16)
    km = a[:, mw:2 * mw]
    vm = a[:, 2 * mw:]
    km_ref[...] = km
    vm_ref[...] = vm
    kmb_ref[...] = km.astype(BF16)
    vmb_ref[...] = vm.astype(BF16)

    a = jax.nn.sigmoid(_dot(h, wg_ref[...]))
    sgf_ref[...] = a[:, :d].astype(BF16)
    sgm_ref[...] = a[:, d:].astype(BF16)


def _inproj(x, shift, scale, g, wf, wl, wm, wg, bl, *, tm, nh):
    b, s, d = x.shape
    fw = wf.shape[1] // 3
    mw = wm.shape[1] // 3
    mrows = shift.shape[1]
    if mrows == 1:
        mod_spec = pl.BlockSpec((None, 1, d), lambda bi, i: (bi, 0, 0))
    else:
        mod_spec = pl.BlockSpec((None, tm, d), lambda bi, i: (bi, i, 0))
    row = lambda w: pl.BlockSpec((None, tm, w), lambda bi, i: (bi, i, 0))
    outs = [(fw, BF16), (fw, F32), (fw, F32), (fw, BF16), (fw, BF16), (nh, F32), None,
            (mw, BF16), (mw, F32), (mw, F32), (mw, BF16), (mw, BF16), (d, BF16), (d, BF16)]
    out_shape, out_specs = [], []
    for o in outs:
        if o is None:
            out_shape.append(jax.ShapeDtypeStruct((b, nh, s), F32))
            out_specs.append(pl.BlockSpec((None, nh, tm), lambda bi, i: (bi, 0, i)))
        else:
            out_shape.append(jax.ShapeDtypeStruct((b, s, o[0]), o[1]))
            out_specs.append(row(o[0]))
    return pl.pallas_call(
        functools.partial(_inproj_kernel, fw=fw, mw=mw, d=d, nh=nh),
        grid=(b, s // tm),
        in_specs=[row(d), mod_spec, mod_spec, _const_spec((1, d)),
                  _const_spec(wf.shape), _const_spec(wl.shape), _const_spec(wm.shape),
                  _const_spec(wg.shape), _const_spec((1, LANES))],
        out_specs=out_specs,
        out_shape=out_shape,
        compiler_params=_params("arbitrary", "arbitrary"),
        name="in_proj",
    )(x, shift, scale, g, wf, wl, wm, wg, bl)


def _scan_lanes(x, reverse):
    n = x.shape[-1]
    lane = lax.broadcasted_iota(jnp.int32, x.shape, x.ndim - 1)
    sh = 1
    while sh < n:
        if reverse:
            x = x + jnp.where(lane < n - sh, pltpu.roll(x, n - sh, axis=x.ndim - 1), 0.0)
        else:
            x = x + jnp.where(lane >= sh, pltpu.roll(x, sh, axis=x.ndim - 1), 0.0)
        sh *= 2
    return x


def _cumsum_kernel(x_ref, o_ref):
    o_ref[...] = _scan_lanes(x_ref[...], reverse=False)


def _cumsum(x):
    b, nh, s = x.shape
    return pl.pallas_call(
        _cumsum_kernel,
        grid=(b,),
        in_specs=[pl.BlockSpec((None, nh, s), lambda bi: (bi, 0, 0))],
        out_specs=pl.BlockSpec((None, nh, s), lambda bi: (bi, 0, 0)),
        out_shape=jax.ShapeDtypeStruct((b, nh, s), F32),
        compiler_params=_params("arbitrary"),
        name="logf_cumsum",
    )(x)


def _fox_aug_kernel(c_ref, aq_ref, ak_ref):
    c = c_ref[...] * LOG2E
    nh, tm = c.shape
    pieces = _split3(c)
    sub = lax.broadcasted_iota(jnp.int32, (LANES, tm), 0)
    row = lambda n, h: jnp.broadcast_to(pieces[n][h:h + 1, :], (LANES, tm))
    for h in range(nh):
        o = SUBLANES * (h % 2)
        mq = jnp.where((sub >= o + 3) & (sub < o + 6), 1.0, 0.0)
        for n in range(3):
            mq = jnp.where(sub == o + n, row(n, h), mq)
        aq_ref[:, h * LANES:(h + 1) * LANES] = mq.T.astype(BF16)
    for hp in range(nh // 2):
        mk = jnp.zeros((LANES, tm), F32)
        for hh in range(2):
            o = SUBLANES * hh
            mk = jnp.where((sub >= o) & (sub < o + 3), 1.0, mk)
            for n in range(3):
                mk = jnp.where(sub == o + 3 + n, -row(n, 2 * hp + hh), mk)
        ak_ref[:, hp * LANES:(hp + 1) * LANES] = mk.T.astype(BF16)


def _fox_aug(cum, *, tm):
    b, nh, s = cum.shape
    return pl.pallas_call(
        _fox_aug_kernel,
        grid=(b, s // tm),
        in_specs=[pl.BlockSpec((None, nh, tm), lambda bi, i: (bi, 0, i))],
        out_specs=[pl.BlockSpec((None, tm, nh * LANES), lambda bi, i: (bi, i, 0)),
                   pl.BlockSpec((None, tm, nh // 2 * LANES), lambda bi, i: (bi, i, 0))],
        out_shape=[jax.ShapeDtypeStruct((b, s, nh * LANES), BF16),
                   jax.ShapeDtypeStruct((b, s, nh // 2 * LANES), BF16)],
        compiler_params=_params("arbitrary", "arbitrary"),
        name="fox_aug",
    )(cum)


def _kmean_kernel(k_ref, o_ref, *, nvalid):
    i = pl.program_id(1)

    @pl.when(i < nvalid)
    def _():
        k = k_ref[...]
        o_ref[...] = jnp.mean(k.reshape(SUBLANES, MOBA_BLOCK, k.shape[-1]), axis=1)

    @pl.when(i >= nvalid)
    def _():
        o_ref[...] = jnp.zeros_like(o_ref)


def _kmean(k, nbp):
    b, s, w = k.shape
    rows = SUBLANES * MOBA_BLOCK
    nvalid = s // rows
    return pl.pallas_call(
        functools.partial(_kmean_kernel, nvalid=nvalid),
        grid=(b, nbp // SUBLANES),
        in_specs=[pl.BlockSpec((None, rows, w), lambda bi, i: (bi, jnp.minimum(i, nvalid - 1), 0))],
        out_specs=pl.BlockSpec((None, SUBLANES, w), lambda bi, i: (bi, i, 0)),
        out_shape=jax.ShapeDtypeStruct((b, nbp, w), F32),
        compiler_params=_params("arbitrary", "arbitrary"),
        name="moba_kmean",
    )(k)


def _pick_topk(g, idx, nbp):
    sel = jnp.zeros(g.shape, jnp.bool_)
    for _ in range(MOBA_TOPK):
        mx = jnp.max(g, axis=0, keepdims=True)
        is_max = (g == mx) & (mx > -jnp.inf)
        first = jnp.min(jnp.where(is_max, idx, nbp), axis=0, keepdims=True)
        pick = idx == first
        sel = sel | pick
        g = jnp.where(pick, -jnp.inf, g)
    return sel


def _moba_aug_kernel(slope_ref, q_ref, km_ref, aq_ref):
    t = MOBA_BLOCK
    hp = pl.program_id(1)
    i = pl.program_id(2)
    q2 = q_ref[...]
    km = km_ref[...]
    km_hi = km.astype(BF16)
    km_lo = (km - km_hi.astype(F32)).astype(BF16)
    lane = lax.broadcasted_iota(jnp.int32, (1, LANES), 1)
    blk = lax.broadcasted_iota(jnp.int32, (t, LANES), 1)
    blk_t = lax.broadcasted_iota(jnp.int32, (LANES, t), 0)
    qpos = (i * t + lax.broadcasted_iota(jnp.int32, (t, LANES), 0)).astype(F32)
    zero = jnp.zeros_like(q2)
    for hh in range(2):
        sig = jnp.full((t, LANES), slope_ref[2 * hp + hh] * LOG2E, F32)
        qh = jnp.where((lane < HEAD_DIM) if hh == 0 else (lane >= HEAD_DIM), q2, zero)
        gate = _dot_nt(km_hi, qh) + _dot_nt(km_lo, qh)
        sel = _pick_topk(jnp.where(blk_t < i, gate, -jnp.inf), blk_t, LANES)
        aug = jnp.where(sel | (blk_t == i), 0.0, -BIG).T
        s3 = _split3(sig)
        t3 = _split3(-sig * qpos)
        for n in range(3):
            aug = jnp.where((blk == A_BLK + n) | (blk == A_OFF + n), s3[n], aug)
            aug = jnp.where(blk == A_QPOS + n, t3[n], aug)
        aug = jnp.where(blk >= A_END, 0.0, aug)
        aq_ref[:, hh * LANES:(hh + 1) * LANES] = aug.astype(BF16)


def _moba_aug(slopes, q, kmean):
    b, s, w = q.shape
    npair = w // LANES
    t = MOBA_BLOCK
    return pl.pallas_call(
        _moba_aug_kernel,
        grid=(b, npair, s // t),
        in_specs=[pl.BlockSpec(memory_space=pltpu.SMEM),
                  pl.BlockSpec((None, t, LANES), lambda bi, hp, i: (bi, i, hp)),
                  pl.BlockSpec((None, LANES, LANES), lambda bi, hp, i: (bi, 0, hp))],
        out_specs=pl.BlockSpec((None, t, 2 * LANES), lambda bi, hp, i: (bi, i, hp)),
        out_shape=jax.ShapeDtypeStruct((b, s, 2 * w), BF16),
        compiler_params=_params("arbitrary", "arbitrary", "arbitrary"),
        name="moba_aug",
    )(slopes, q, kmean)


def _moba_key_table(s):
    pos = jnp.arange(s, dtype=jnp.int32)[:, None]
    lane = jnp.arange(LANES, dtype=jnp.int32)[None, :]
    blk = pos // MOBA_BLOCK
    tab = jnp.where(lane == blk, 1, 0)
    tab = jnp.where((lane >= A_BLK) & (lane < A_OFF), MOBA_BLOCK * blk, tab)
    tab = jnp.where((lane >= A_OFF) & (lane < A_QPOS), pos % MOBA_BLOCK, tab)
    tab = jnp.where((lane >= A_QPOS) & (lane < A_END), 1, tab)
    return tab.astype(BF16)[None]


def _attn_kernel(q_ref, aq_ref, k_ref, ak_ref, v_ref, o_ref, *, t, block_causal):
    i = pl.program_id(2)
    q2 = q_ref[...]
    lane = lax.broadcasted_iota(jnp.int32, (1, LANES), 1)
    zero = jnp.zeros_like(q2)
    qa = (jnp.concatenate([jnp.where(lane < HEAD_DIM, q2, zero), aq_ref[:, :LANES]], axis=1),
          jnp.concatenate([jnp.where(lane >= HEAD_DIM, q2, zero), aq_ref[:, LANES:]], axis=1))
    ones = jnp.ones((t, LANES), BF16)

    def step(s, va, m, acc):
        part = s[:, :LANES]
        for c in range(1, s.shape[1] // LANES):
            part = jnp.maximum(part, s[:, c * LANES:(c + 1) * LANES])
        m_new = jnp.maximum(m, jnp.max(part, axis=-1, keepdims=True))
        p = jnp.exp2(s - m_new)
        acc = jnp.exp2(m - m_new) * acc + _dot(p.astype(BF16), va)
        return m_new, acc

    def tiles(j):
        ks = pl.multiple_of(j * t, t)
        ka = jnp.concatenate([k_ref[pl.ds(ks, t), :], ak_ref[pl.ds(ks, t), :]], axis=1)
        v = v_ref[pl.ds(ks, t), :]
        return ka, (jnp.where(lane < HEAD_DIM, v, ones), jnp.where(lane >= HEAD_DIM, v, ones))

    def body(j, carry):
        ka, va = tiles(j)
        return tuple(step(_dot_nt(qa[hh], ka), va[hh], *carry[hh]) for hh in range(2))

    init = (jnp.full((t, 1), -jnp.inf, F32), jnp.zeros((t, LANES), F32))
    carry = lax.fori_loop(0, i, body, (init, init))

    ka, va = tiles(i)
    row = lax.broadcasted_iota(jnp.int32, (t, t), 0)
    col = lax.broadcasted_iota(jnp.int32, (t, t), 1)
    future = col > row
    if block_causal:
        future = future & ((col // MOBA_BLOCK) == (row // MOBA_BLOCK))
    outs = []
    for hh in range(2):
        s = jnp.where(future, -jnp.inf, _dot_nt(qa[hh], ka))
        _, acc = step(s, va[hh], *carry[hh])
        outs.append(acc / pltpu.roll(acc, HEAD_DIM, axis=1))
    o_ref[...] = jnp.where(lane < HEAD_DIM, outs[0], outs[1]).astype(BF16)


def _attention(q, aq, k, ak, v, *, t, block_causal, name):
    b, s, w = q.shape
    npair = w // LANES
    per_seq = ak.shape[0] > 1
    per_pair = ak.shape[2] > LANES
    return pl.pallas_call(
        functools.partial(_attn_kernel, t=t, block_causal=block_causal),
        grid=(b, npair, s // t),
        in_specs=[pl.BlockSpec((None, t, LANES), lambda bi, hp, i: (bi, i, hp)),
                  pl.BlockSpec((None, t, 2 * LANES), lambda bi, hp, i: (bi, i, hp)),
                  pl.BlockSpec((None, s, LANES), lambda bi, hp, i: (bi, 0, hp)),
                  pl.BlockSpec((None, s, LANES),
                               lambda bi, hp, i: (bi if per_seq else 0, 0, hp if per_pair else 0)),
                  pl.BlockSpec((None, s, LANES), lambda bi, hp, i: (bi, 0, hp))],
        out_specs=pl.BlockSpec((None, t, LANES), lambda bi, hp, i: (bi, i, hp)),
        out_shape=jax.ShapeDtypeStruct((b, s, w), BF16),
        compiler_params=_params("arbitrary", "arbitrary", "arbitrary"),
        name=name,
    )(q, aq, k, ak, v)


def _outproj_kernel(x_ref, of_ref, om_ref, sgf_ref, sgm_ref, gate_ref, g_ref, wbf_ref, wbm_ref, wo_ref, y_ref):
    yf = _dot(of_ref[...], wbf_ref[...])
    ym = _dot(om_ref[...], wbm_ref[...])
    mixed = sgf_ref[...].astype(F32) * yf + sgm_ref[...].astype(F32) * ym
    z = _dot(mixed.astype(BF16), wo_ref[...])
    y_ref[...] = x_ref[...] + gate_ref[...] * _rms(z, g_ref[...])


def _outproj(x, of, om, sgf, sgm, gate, g, wbf, wbm, wo, *, tm):
    b, s, d = x.shape
    if gate.shape[1] == 1:
        mod_spec = pl.BlockSpec((None, 1, d), lambda bi, i: (bi, 0, 0))
    else:
        mod_spec = pl.BlockSpec((None, tm, d), lambda bi, i: (bi, i, 0))
    row = lambda w: pl.BlockSpec((None, tm, w), lambda bi, i: (bi, i, 0))
    return pl.pallas_call(
        _outproj_kernel,
        grid=(b, s // tm),
        in_specs=[row(d), row(of.shape[-1]), row(om.shape[-1]), row(d), row(d), mod_spec,
                  _const_spec((1, d)), _const_spec(wbf.shape), _const_spec(wbm.shape), _const_spec(wo.shape)],
        out_specs=row(d),
        out_shape=jax.ShapeDtypeStruct((b, s, d), F32),
        compiler_params=_params("arbitrary", "arbitrary"),
        name="out_proj",
    )(x, of, om, sgf, sgm, gate, g, wbf, wbm, wo)


def _gelu_tanh(x):
    c = 0.7978845608028654
    return 0.5 * x * (1.0 + jnp.tanh(c * (x + 0.044715 * (x * x * x))))


def _ffn_kernel(*refs, dff, ch, tm, stepwise):
    if stepwise:
        (x_ref, shift_ref, scale_ref, gate_ref, gpre_ref, gpost_ref, wup_ref, wc_ref, bc_ref, wdn_ref,
         p2_ref, p1_ref, y_ref, u_ref, acc_ref) = refs
    else:
        (x_ref, shift_ref, scale_ref, gate_ref, gpre_ref, gpost_ref, wup_ref, wc_ref, bc_ref, wdn_ref,
         y_ref, tail_ref, acc_ref, ubuf_ref, carry_ref) = refs
        first = pl.program_id(1) == 0
    x = x_ref[...]
    h = (_rms(x, gpre_ref[...]) * (1.0 + scale_ref[...]) + shift_ref[...]).astype(BF16)
    acc_ref[...] = jnp.zeros_like(acc_ref)
    for c in range(dff // ch):
        cs = slice(c * ch, (c + 1) * ch)
        u = _dot(h, wup_ref[:, cs])
        v = _dot(h, wup_ref[:, dff + c * ch:dff + (c + 1) * ch])
        if stepwise:
            u2 = p2_ref[:, cs]
            u1 = p1_ref[:, cs]
            u_ref[:, cs] = u
        else:
            @pl.when(first)
            def _():
                carry_ref[c] = jnp.zeros((SUBLANES, ch), F32)

            ubuf_ref[0:SUBLANES, :] = carry_ref[c]
            ubuf_ref[SUBLANES:SUBLANES + tm, :] = u
            u1 = ubuf_ref[SUBLANES - 1:SUBLANES - 1 + tm, :]
            u2 = ubuf_ref[SUBLANES - 2:SUBLANES - 2 + tm, :]
            tail = u[tm - SUBLANES:, :]
            carry_ref[c] = tail
            tail_ref[:, cs] = tail
        uc = bc_ref[:, cs] + (wc_ref[0:1, cs] * u2 + wc_ref[1:2, cs] * u1 + wc_ref[2:3, cs] * u)
        g = (_gelu_tanh(uc) * v).astype(BF16)
        acc_ref[...] += _dot(g, wdn_ref[cs, :])
    y_ref[...] = x + gate_ref[...] * _rms(acc_ref[...], gpost_ref[...])


def _ffn(x, shift, scale, gate, gpre, gpost, wup, wc, bc, wdn, prev=None, *, tm):
    b, s, d = x.shape
    dff = wdn.shape[0]
    ch = 256 if dff % 256 == 0 else LANES
    stepwise = prev is not None
    if shift.shape[1] == 1:
        mod_spec = pl.BlockSpec((None, 1, d), lambda bi, i: (bi, 0, 0))
    else:
        mod_spec = pl.BlockSpec((None, tm, d), lambda bi, i: (bi, i, 0))
    row = lambda w: pl.BlockSpec((None, tm, w), lambda bi, i: (bi, i, 0))
    in_specs = [row(d), mod_spec, mod_spec, mod_spec, _const_spec((1, d)), _const_spec((1, d)),
                _const_spec(wup.shape), _const_spec(wc.shape), _const_spec(bc.shape), _const_spec(wdn.shape)]
    args = [x, shift, scale, gate, gpre, gpost, wup, wc, bc, wdn]
    scratch = [pltpu.VMEM((tm, d), F32)]
    if stepwise:
        in_specs += [row(dff), row(dff)]
        args += list(prev)
        out_specs = [row(d), row(dff)]
        out_shape = [jax.ShapeDtypeStruct((b, s, d), F32), jax.ShapeDtypeStruct((b, s, dff), F32)]
    else:
        out_specs = [row(d), pl.BlockSpec((None, SUBLANES, dff), lambda bi, i: (bi, 0, 0))]
        out_shape = [jax.ShapeDtypeStruct((b, s, d), F32), jax.ShapeDtypeStruct((b, SUBLANES, dff), F32)]
        scratch += [pltpu.VMEM((tm + SUBLANES, ch), F32), pltpu.VMEM((dff // ch, SUBLANES, ch), F32)]
    return pl.pallas_call(
        functools.partial(_ffn_kernel, dff=dff, ch=ch, tm=tm, stepwise=stepwise),
        grid=(b, s // tm),
        in_specs=in_specs,
        out_specs=out_specs,
        out_shape=out_shape,
        scratch_shapes=scratch,
        compiler_params=_params("arbitrary", "arbitrary"),
        name="ffn_step" if stepwise else "ffn_seq",
    )(*args)


def _page_spec(block, layer, npages_per_seq, per_step, p, seq_of=lambda bi: bi):
    def index(bi, g, pt_ref):
        page = pt_ref[seq_of(bi) * npages_per_seq + g * per_step + p]
        return (layer, page) + (0,) * (len(block) - 2)
    return pl.BlockSpec(block, index)


def _col_spec(nh, hd, seq_of=lambda bi: bi):
    return pl.BlockSpec((None, nh, hd, 1), lambda bi, g, pt: (seq_of(bi), 0, 0, 0))


ROW_BLK = 2 * SUBLANES


def _rows_spec(w, seq_of=lambda bi: bi):
    return pl.BlockSpec((None, ROW_BLK, w), lambda bi, g, pt: (0, seq_of(bi) // ROW_BLK, 0))


def _head_rows(ref, seq, nh):
    row = ref[pl.ds(seq % ROW_BLK, 1), :]
    hd = row.shape[1] // nh
    return [row[:, h * hd:(h + 1) * hd] for h in range(nh)]


def _row_to_col(row):
    n = row.shape[1]
    eye = lax.broadcasted_iota(jnp.int32, (n, n), 0) == lax.broadcasted_iota(jnp.int32, (n, n), 1)
    return jnp.sum(jnp.where(eye, jnp.broadcast_to(row, (n, n)), 0.0), axis=1, keepdims=True)


def _stack_rows(rows):
    shape = (len(rows), rows[0].shape[1])
    sub = lax.broadcasted_iota(jnp.int32, shape, 0)
    out = jnp.broadcast_to(rows[0], shape)
    for h in range(1, len(rows)):
        out = jnp.where(sub == h, jnp.broadcast_to(rows[h], shape), out)
    return out


def _page_scores(k_ref, qb_ref):
    nh = k_ref.shape[0]
    return _stack_rows([jnp.sum(k_ref[h] * qb_ref[h], axis=0, keepdims=True) for h in range(nh)])


def _logf_bias_kernel(pt_ref, new_ref, *refs, per_step):
    pages = refs[:per_step]
    o_ref, x_ref = refs[per_step:]
    g = pl.program_id(1)
    rows = pages[0].shape[-1]
    for p in range(per_step):
        x_ref[:, pl.ds(pl.multiple_of((g * per_step + p) * rows, rows), rows)] = pages[p][...]

    @pl.when(g == pl.num_programs(1) - 1)
    def _():
        x = x_ref[...]
        n = x.shape[-1]
        lane = lax.broadcasted_iota(jnp.int32, x.shape, 1)
        shifted = jnp.where(lane < n - 1, pltpu.roll(x, n - 1, axis=1), 0.0)
        o_ref[...] = (_scan_lanes(shifted, reverse=True) + new_ref[...]) * LOG2E


def _logf_bias(cache_t, layer, new, pt_flat, npages, per_step):
    _, _, nh, rows = cache_t.shape
    b = new.shape[0]
    n = npages * rows
    grid_spec = pltpu.PrefetchScalarGridSpec(
        num_scalar_prefetch=1,
        grid=(b, npages // per_step),
        in_specs=[pl.BlockSpec((None, nh, 1), lambda bi, g, pt: (bi, 0, 0))]
                 + [_page_spec((None, None, nh, rows), layer, npages, per_step, p) for p in range(per_step)],
        out_specs=pl.BlockSpec((None, nh, n), lambda bi, g, pt: (bi, 0, 0)),
        scratch_shapes=[pltpu.VMEM((nh, n), F32)],
    )
    return pl.pallas_call(
        functools.partial(_logf_bias_kernel, per_step=per_step),
        grid_spec=grid_spec,
        out_shape=jax.ShapeDtypeStruct((b, nh, n), F32),
        compiler_params=_params("arbitrary", "arbitrary"),
        name="logf_bias",
    )(pt_flat, new, *([cache_t] * per_step))


def _fox_dec_kernel(pt_ref, q_ref, kn_ref, vn_ref, d_ref, *refs, per_step):
    kp = refs[:per_step]
    vp = refs[per_step:2 * per_step]
    o_ref, qb_ref, m_ref, l_ref, acc_ref = refs[2 * per_step:]
    seq = pl.program_id(0)
    g = pl.program_id(1)
    nh, hd, rows = kp[0].shape

    @pl.when(g == 0)
    def _():
        q = _head_rows(q_ref, seq, nh)
        for h in range(nh):
            qb_ref[h] = jnp.broadcast_to(_row_to_col(q[h]), (hd, rows))
        m_ref[...] = jnp.full_like(m_ref, -BIG)
        l_ref[...] = jnp.zeros_like(l_ref)
        acc_ref[...] = jnp.zeros_like(acc_ref)

    ss = []
    m_new = m_ref[...]
    for p in range(per_step):
        s = _page_scores(kp[p], qb_ref) + d_ref[:, p * rows:(p + 1) * rows]
        ss.append(s)
        m_new = jnp.maximum(m_new, jnp.max(s, axis=-1, keepdims=True))
    alpha = jnp.exp2(m_ref[...] - m_new)
    l = alpha * l_ref[...]
    prs = []
    for p in range(per_step):
        pr = jnp.exp2(ss[p] - m_new)
        prs.append(pr)
        l = l + jnp.sum(pr, axis=-1, keepdims=True)
    for h in range(nh):
        acc = alpha[h:h + 1, :] * acc_ref[h]
        for p in range(per_step):
            acc = acc + vp[p][h] * prs[p][h:h + 1, :]
        acc_ref[h] = acc
    m_ref[...] = m_new
    l_ref[...] = l

    @pl.when(g == pl.num_programs(1) - 1)
    def _():
        q, kn, vn = _head_rows(q_ref, seq, nh), _head_rows(kn_ref, seq, nh), _head_rows(vn_ref, seq, nh)
        for h in range(nh):
            s_self = jnp.sum(q[h] * kn[h], axis=1, keepdims=True)
            m_h = m_new[h:h + 1, :]
            m_fin = jnp.maximum(m_h, s_self)
            a = jnp.exp2(m_h - m_fin)
            p_self = jnp.exp2(s_self - m_fin)
            num = a * jnp.sum(acc_ref[h], axis=-1, keepdims=True) + p_self * _row_to_col(vn[h])
            o_ref[h] = num / (a * l[h:h + 1, :] + p_self)


def _fox_decode(q, kn, vn, bias, cache_k, cache_v, layer, pt_flat, npages, per_step):
    b, w = q.shape[1], q.shape[2]
    _, _, nh, hd, rows = cache_k.shape
    page = lambda p: _page_spec((None, None, nh, hd, rows), layer, npages, per_step, p)
    grid_spec = pltpu.PrefetchScalarGridSpec(
        num_scalar_prefetch=1,
        grid=(b, npages // per_step),
        in_specs=[_rows_spec(w), _rows_spec(w), _rows_spec(w),
                  pl.BlockSpec((None, nh, per_step * rows), lambda bi, g, pt: (bi, 0, g))]
                 + [page(p) for p in range(per_step)] + [page(p) for p in range(per_step)],
        out_specs=_col_spec(nh, hd),
        scratch_shapes=[pltpu.VMEM((nh, hd, rows), F32), pltpu.VMEM((nh, 1), F32), pltpu.VMEM((nh, 1), F32),
                        pltpu.VMEM((nh, hd, rows), F32)],
    )
    return pl.pallas_call(
        functools.partial(_fox_dec_kernel, per_step=per_step),
        grid_spec=grid_spec,
        out_shape=jax.ShapeDtypeStruct((b, nh, hd, 1), F32),
        compiler_params=_params("arbitrary", "arbitrary"),
        name="fox_decode",
    )(pt_flat, q, kn, vn, bias, *([cache_k] * per_step), *([cache_v] * per_step))


def _moba_dec_kernel(pt_ref, slope_ref, q_ref, kn_ref, vn_ref, *refs, per_step, pages_per_block, nblocks):
    kp = refs[:per_step]
    vp = refs[per_step:2 * per_step]
    o_ref, qb_ref, s_ref, gate_ref, w_ref, l_ref, acc_ref = refs[2 * per_step:]
    bb = pl.program_id(0)
    g = pl.program_id(1)
    nseq = pl.num_programs(0) - 1
    ng = pl.num_programs(1)
    nh, hd, rows = kp[0].shape
    npages = nblocks * pages_per_block
    past = npages * rows
    lane = lax.broadcasted_iota(jnp.int32, (nh, LANES), 1)
    slot = bb % 2
    prev = 1 - slot
    keys_on = bb < nseq
    vals_on = bb >= 1
    kseq = jnp.minimum(bb, nseq - 1)
    vseq = jnp.maximum(bb - 1, 0)

    @pl.when(keys_on & (g == 0))
    def _():
        q = _head_rows(q_ref, kseq, nh)
        for h in range(nh):
            qb_ref[h] = jnp.broadcast_to(_row_to_col(q[h]), (hd, rows))
        gate_ref[...] = jnp.zeros_like(gate_ref)

    @pl.when(keys_on)
    def _():
        gate = gate_ref[...]
        for p in range(per_step):
            page = g * per_step + p
            s = _page_scores(kp[p], qb_ref)
            s_ref[slot, :, pl.ds(pl.multiple_of(page * rows, rows), rows)] = s
            gate = gate + jnp.where(lane == page // pages_per_block, jnp.sum(s, axis=-1, keepdims=True), 0.0)
        gate_ref[...] = gate

    @pl.when(keys_on & (g == ng - 1))
    def _():
        gsel = jnp.where(lane < nblocks, gate_ref[...], -jnp.inf)
        chosen = jnp.zeros((nh, LANES), F32)
        for _ in range(MOBA_TOPK):
            mx = jnp.max(gsel, axis=-1, keepdims=True)
            first = jnp.min(jnp.where(gsel == mx, lane, LANES), axis=-1, keepdims=True)
            chosen = jnp.where(lane == first, 1.0, chosen)
            gsel = jnp.where(lane == first, -jnp.inf, gsel)
        sub = lax.broadcasted_iota(jnp.int32, (nh, 1), 0)
        sig = jnp.zeros((nh, 1), F32)
        for h in range(nh):
            sig = jnp.where(sub == h, slope_ref[h] * LOG2E, sig)
        picked = jnp.concatenate(
            [jnp.broadcast_to(jnp.sum(jnp.where(lane == n, chosen, 0.0), axis=-1, keepdims=True),
                              (nh, MOBA_BLOCK)) for n in range(nblocks)], axis=1)
        pos = lax.broadcasted_iota(jnp.int32, (nh, past), 1)
        sb = jnp.where(picked > 0.0, s_ref[slot] - sig * (past - pos).astype(F32), -BIG)
        q, kn = _head_rows(q_ref, kseq, nh), _head_rows(kn_ref, kseq, nh)
        s_self = _stack_rows([jnp.sum(q[h] * kn[h], axis=1, keepdims=True) for h in range(nh)])
        m = jnp.maximum(jnp.max(sb, axis=-1, keepdims=True), s_self)
        pr = jnp.exp2(sb - m)
        w_self = jnp.exp2(s_self - m)
        s_ref[slot] = pr
        w_ref[slot] = w_self
        l_ref[slot] = jnp.sum(pr, axis=-1, keepdims=True) + w_self

    @pl.when(vals_on & (g == 0))
    def _():
        acc_ref[...] = jnp.zeros_like(acc_ref)

    @pl.when(vals_on)
    def _():
        for h in range(nh):
            acc = acc_ref[h]
            for p in range(per_step):
                page = g * per_step + p
                pr = s_ref[prev, pl.ds(h, 1), pl.ds(pl.multiple_of(page * rows, rows), rows)]
                acc = acc + vp[p][h] * pr
            acc_ref[h] = acc

    @pl.when(vals_on & (g == ng - 1))
    def _():
        w_self = w_ref[prev]
        l = l_ref[prev]
        vn = _head_rows(vn_ref, vseq, nh)
        for h in range(nh):
            num = jnp.sum(acc_ref[h], axis=-1, keepdims=True) + w_self[h:h + 1, :] * _row_to_col(vn[h])
            o_ref[h] = num / l[h:h + 1, :]


def _moba_decode(q, kn, vn, slopes, cache_k, cache_v, layer, pt_flat, npages, per_step):
    b, w = q.shape[1], q.shape[2]
    _, _, nh, hd, rows = cache_k.shape
    ng = npages // per_step
    pages_per_block = MOBA_BLOCK // rows
    kseq = lambda bi: jnp.minimum(bi, b - 1)
    vseq = lambda bi: jnp.maximum(bi - 1, 0)
    kpage = lambda p: _page_spec((None, None, nh, hd, rows), layer, npages, per_step, p, seq_of=kseq)
    vpage = lambda p: _page_spec((None, None, nh, hd, rows), layer, npages, per_step, p, seq_of=vseq)
    grid_spec = pltpu.PrefetchScalarGridSpec(
        num_scalar_prefetch=1,
        grid=(b + 1, ng),
        in_specs=[pl.BlockSpec(memory_space=pltpu.SMEM), _rows_spec(w, kseq), _rows_spec(w, kseq),
                  _rows_spec(w, vseq)]
                 + [kpage(p) for p in range(per_step)] + [vpage(p) for p in range(per_step)],
        out_specs=_col_spec(nh, hd, vseq),
        scratch_shapes=[pltpu.VMEM((nh, hd, rows), F32), pltpu.VMEM((2, nh, npages * rows), F32),
                        pltpu.VMEM((nh, LANES), F32), pltpu.VMEM((2, nh, 1), F32), pltpu.VMEM((2, nh, 1), F32),
                        pltpu.VMEM((nh, hd, rows), F32)],
    )
    return pl.pallas_call(
        functools.partial(_moba_dec_kernel, per_step=per_step, pages_per_block=pages_per_block,
                          nblocks=npages // pages_per_block),
        grid_spec=grid_spec,
        out_shape=jax.ShapeDtypeStruct((b, nh, hd, 1), F32),
        compiler_params=_params("arbitrary", "arbitrary"),
        name="moba_decode",
    )(pt_flat, slopes, q, kn, vn, *([cache_k] * per_step), *([cache_v] * per_step))


def _split_w_in(w_in, nh_fox, fw, mw, d):
    o1 = 3 * fw
    o2 = o1 + nh_fox
    o3 = o2 + 3 * mw
    wf = w_in[:, :o1].astype(BF16)
    wl = jnp.pad(w_in[:, o1:o2], ((0, 0), (0, LANES - nh_fox))).astype(BF16)
    wm = w_in[:, o2:o3].astype(BF16)
    wg = w_in[:, o3:].astype(BF16)
    return wf, wl, wm, wg


def kernel(x_prompt, x_sample, cache_fox_k, cache_fox_v, cache_fox_logf, cache_moba_k, cache_moba_v,
           state_conv, page_table, c_prompt, c_sample, w_ada, b_ada, g_mix_pre, g_mix_post, w_in, b_forget,
           w_branch_fox, w_branch_moba, w_out, g_ffn_pre, g_ffn_post, w_up, w_conv, b_conv, w_down):
    depth = w_ada.shape[0]
    bp, s, d = x_prompt.shape
    bs, dec_s, _ = x_sample.shape
    nh_fox = b_forget.shape[1]
    fw = w_branch_fox.shape[1]
    mw = w_branch_moba.shape[1]
    nh_moba = mw // HEAD_DIM
    page_rows = cache_fox_k.shape[2]
    npages = page_table.shape[1]
    past = npages * page_rows
    tm = 512
    t_attn = 1024
    per_step = min(16, npages)
    assert dec_s == 1 and fw == nh_fox * HEAD_DIM and nh_fox == SUBLANES and nh_moba == SUBLANES
    assert fw % LANES == 0 and mw % LANES == 0 and s % (SUBLANES * MOBA_BLOCK) == 0 and s % tm == 0
    assert s // MOBA_BLOCK <= SEL_LANES and s % t_attn == 0 and t_attn % MOBA_BLOCK == 0
    assert page_rows == LANES and MOBA_BLOCK % page_rows == 0 and npages % per_step == 0 and bs % ROW_BLK == 0
    assert past % MOBA_BLOCK == 0 and MOBA_TOPK <= past // MOBA_BLOCK <= LANES
    slopes = jnp.asarray([2.0 ** (-8.0 * (h + 1) / nh_moba) for h in range(nh_moba)], F32)
    pt_flat = page_table.reshape(-1).astype(jnp.int32)
    moba_keys = _moba_key_table(s)
    fk_t, fv_t = jnp.transpose(cache_fox_k, (0, 1, 3, 4, 2)), jnp.transpose(cache_fox_v, (0, 1, 3, 4, 2))
    mk_t, mv_t = jnp.transpose(cache_moba_k, (0, 1, 3, 4, 2)), jnp.transpose(cache_moba_v, (0, 1, 3, 4, 2))
    fl_t = jnp.transpose(cache_fox_logf, (0, 1, 3, 2))

    rows_c = bp + bs
    rows_pad = -(-rows_c // SUBLANES) * SUBLANES
    c_all = jnp.pad(jnp.concatenate([c_prompt, c_sample], axis=0), ((0, rows_pad - rows_c), (0, 0)))

    y_p, y_s = x_prompt, x_sample.reshape(1, bs, d)
    outs_p, outs_s = [], []
    for l in range(depth):
        mod = _ada(c_all, w_ada[l], b_ada[l][None, :]).reshape(rows_pad, N_MOD, d)
        mod_p = [mod[:bp, i][:, None, :] for i in range(N_MOD)]
        mod_s = [mod[bp:rows_c, i][None] for i in range(N_MOD)]
        wf, wl, wm, wg = _split_w_in(w_in[l], nh_fox, fw, mw, d)
        bl = jnp.pad(b_forget[l], (0, LANES - nh_fox))[None, :]
        g1, g2 = g_mix_pre[l][None, :], g_mix_post[l][None, :]
        g3, g4 = g_ffn_pre[l][None, :], g_ffn_post[l][None, :]
        wbf, wbm, wo = w_branch_fox[l].astype(BF16), w_branch_moba[l].astype(BF16), w_out[l].astype(BF16)
        wup, wdn = w_up[l].astype(BF16), w_down[l].astype(BF16)
        wc, bc = w_conv[l], b_conv[l][None, :]

        (qf, kf, vf, kfb, vfb, lf, lft, qm, km, vm, kmb, vmb, sgf, sgm) = _inproj(
            y_p, mod_p[0], mod_p[1], g1, wf, wl, wm, wg, bl, tm=tm, nh=nh_fox)
        aq_f, ak_f = _fox_aug(_cumsum(lft), tm=tm)
        o_f = _attention(qf, aq_f, kfb, ak_f, vfb, t=t_attn, block_causal=False, name="fox_attention")
        aq_m = _moba_aug(slopes, qm, _kmean(km, LANES))
        o_m = _attention(qm, aq_m, kmb, moba_keys, vmb, t=t_attn, block_causal=True, name="moba_attention")
        x1 = _outproj(y_p, o_f, o_m, sgf, sgm, mod_p[2], g2, wbf, wbm, wo, tm=tm)
        y_p, tail = _ffn(x1, mod_p[3], mod_p[4], mod_p[5], g3, g4, wup, wc, bc, wdn, tm=tm)
        heads = lambda t, n: t.reshape(bp, s, n, HEAD_DIM)
        outs_p.append((heads(kf, nh_fox), heads(vf, nh_fox), lf, heads(km, nh_moba), heads(vm, nh_moba),
                       tail[:, SUBLANES - (CONV_W - 1):]))

        (qf, kf, vf, _, _, lf, _, qm, km, vm, _, _, sgf, sgm) = _inproj(
            y_s, mod_s[0], mod_s[1], g1, wf, wl, wm, wg, bl, tm=bs, nh=nh_fox)
        bias = _logf_bias(fl_t, l, lf.reshape(bs, nh_fox, 1), pt_flat, npages, npages)
        o_f = _fox_decode(qf.astype(F32), kf, vf, bias, fk_t, fv_t, l, pt_flat, npages, per_step)
        o_m = _moba_decode(qm.astype(F32), km, vm, slopes, mk_t, mv_t, l, pt_flat, npages, per_step)
        o_f = o_f.reshape(1, bs, fw).astype(BF16)
        o_m = o_m.reshape(1, bs, mw).astype(BF16)
        x1 = _outproj(y_s, o_f, o_m, sgf, sgm, mod_s[2], g2, wbf, wbm, wo, tm=bs)
        prev = (state_conv[l][None, :, 0], state_conv[l][None, :, 1])
        y_s, u_new = _ffn(x1, mod_s[3], mod_s[4], mod_s[5], g3, g4, wup, wc, bc, wdn, prev, tm=bs)
        heads = lambda t, n: t.reshape(bs, 1, n, HEAD_DIM)
        outs_s.append((heads(kf, nh_fox), heads(vf, nh_fox), lf.reshape(bs, 1, nh_fox),
                       heads(km, nh_moba), heads(vm, nh_moba),
                       jnp.stack([state_conv[l][:, 1], u_new[0]], axis=1)))

    stack = lambda rows, i: jnp.stack([r[i] for r in rows])
    return (y_p, y_s.reshape(bs, 1, d),
            stack(outs_p, 0), stack(outs_p, 1), stack(outs_p, 2), stack(outs_p, 3), stack(outs_p, 4), stack(outs_p, 5),
            stack(outs_s, 0), stack(outs_s, 1), stack(outs_s, 2), stack(outs_s, 3), stack(outs_s, 4), stack(outs_s, 5))
```

```python
import functools

import jax
import jax.numpy as jnp
from jax import lax
from jax.experimental import pallas as pl
from jax.experimental.pallas import tpu as pltpu

F32 = jnp.float32
BF16 = jnp.bfloat16

HEAD_DIM = 64
MOBA_BLOCK = 256
MOBA_TOPK = 3
CONV_W = 3
N_MOD = 6
EPS = 1e-6
LANES = 128
SUBLANES = 8
LOG2E = 1.4426950408889634
BIG = 2.0 ** 100
VMEM_LIMIT = 56 * 1024 * 1024

SEL_LANES = 64
A_BLK = 64
A_OFF = 67
A_QPOS = 70
A_END = 73

_NT = (((1,), (1,)), ((), ()))


def _params(*sem):
    return pltpu.CompilerParams(dimension_semantics=sem, vmem_limit_bytes=VMEM_LIMIT)


def _const_spec(shape):
    nd = len(shape)
    return pl.BlockSpec(shape, lambda *_: (0,) * nd, pipeline_mode=pl.Buffered(1))


def _rms(x, g):
    return x * lax.rsqrt(jnp.mean(x * x, axis=-1, keepdims=True) + EPS) * g


def _dot(a, b):
    return jnp.dot(a, b, preferred_element_type=F32)


def _dot_nt(a, b):
    return lax.dot_general(a, b, _NT, preferred_element_type=F32)


def _split3(x):
    p1 = x.astype(BF16).astype(F32)
    r = x - p1
    p2 = r.astype(BF16).astype(F32)
    p3 = (r - p2).astype(BF16).astype(F32)
    return p1, p2, p3


def _ada_kernel(c_ref, w_ref, b_ref, o_ref):
    c = c_ref[...]
    s = (c * jax.nn.sigmoid(c)).astype(BF16)
    o_ref[...] = _dot(s, w_ref[...].astype(BF16)) + b_ref[...]


def _ada(c, w, b):
    rows, d = c.shape
    n = w.shape[1]
    tn = 1536 if n % 1536 == 0 else n
    return pl.pallas_call(
        _ada_kernel,
        grid=(n // tn,),
        in_specs=[pl.BlockSpec((rows, d), lambda j: (0, 0)),
                  pl.BlockSpec((d, tn), lambda j: (0, j)),
                  pl.BlockSpec((1, tn), lambda j: (0, j))],
        out_specs=pl.BlockSpec((rows, tn), lambda j: (0, j)),
        out_shape=jax.ShapeDtypeStruct((rows, n), F32),
        compiler_params=_params("arbitrary"),
        name="ada_mod",
    )(c, w, b)


def _inproj_kernel(x_ref, shift_ref, scale_ref, g_ref, wf_ref, wl_ref, wm_ref, wg_ref, bl_ref,
                   qf_ref, kf_ref, vf_ref, kfb_ref, vfb_ref, lf_ref, lft_ref,
                   qm_ref, km_ref, vm_ref, kmb_ref, vmb_ref, sgf_ref, sgm_ref, *, fw, mw, d, nh):
    x = x_ref[...]
    h = (_rms(x, g_ref[...]) * (1.0 + scale_ref[...]) + shift_ref[...]).astype(BF16)
    qs = LOG2E * HEAD_DIM ** -0.5

    a = _dot(h, wf_ref[...])
    qf_ref[...] = (a[:, :fw] * qs).astype(BF16)
    kf = a[:, fw:2 * fw]
    vf = a[:, 2 * fw:]
    kf_ref[...] = kf
    vf_ref[...] = vf
    kfb_ref[...] = kf.astype(BF16)
    vfb_ref[...] = vf.astype(BF16)

    z = _dot(h, wl_ref[...]) + bl_ref[...]
    lf = jnp.minimum(z, 0.0) - jnp.log1p(jnp.exp(-jnp.abs(z)))
    lf_ref[...] = lf[:, :nh]
    lft_ref[...] = lf.T[:nh, :]

    a = _dot(h, wm_ref[...])
    qm_ref[...] = (a[:, :mw] * qs).astype(BF16)
    km = a[:, mw:2 * mw]
    vm = a[:, 2 * mw:]
    km_ref[...] = km
    vm_ref[...] = vm
    kmb_ref[...] = km.astype(BF16)
    vmb_ref[...] = vm.astype(BF16)

    a = jax.nn.sigmoid(_dot(h, wg_ref[...]))
    sgf_ref[...] = a[:, :d].astype(BF16)
    sgm_ref[...] = a[:, d:].astype(BF16)


def _inproj(x, shift, scale, g, wf, wl, wm, wg, bl, *, tm, nh):
    b, s, d = x.shape
    fw = wf.shape[1] // 3
    mw = wm.shape[1] // 3
    mrows = shift.shape[1]
    if mrows == 1:
        mod_spec = pl.BlockSpec((None, 1, d), lambda bi, i: (bi, 0, 0))
    else:
        mod_spec = pl.BlockSpec((None, tm, d), lambda bi, i: (bi, i, 0))
    row = lambda w: pl.BlockSpec((None, tm, w), lambda bi, i: (bi, i, 0))
    outs = [(fw, BF16), (fw, F32), (fw, F32), (fw, BF16), (fw, BF16), (nh, F32), None,
            (mw, BF16), (mw, F32), (mw, F32), (mw, BF16), (mw, BF16), (d, BF16), (d, BF16)]
    out_shape, out_specs = [], []
    for o in outs:
        if o is None:
            out_shape.append(jax.ShapeDtypeStruct((b, nh, s), F32))
            out_specs.append(pl.BlockSpec((None, nh, tm), lambda bi, i: (bi, 0, i)))
        else:
            out_shape.append(jax.ShapeDtypeStruct((b, s, o[0]), o[1]))
            out_specs.append(row(o[0]))
    return pl.pallas_call(
        functools.partial(_inproj_kernel, fw=fw, mw=mw, d=d, nh=nh),
        grid=(b, s // tm),
        in_specs=[row(d), mod_spec, mod_spec, _const_spec((1, d)),
                  _const_spec(wf.shape), _const_spec(wl.shape), _const_spec(wm.shape),
                  _const_spec(wg.shape), _const_spec((1, LANES))],
        out_specs=out_specs,
        out_shape=out_shape,
        compiler_params=_params("arbitrary", "arbitrary"),
        name="in_proj",
    )(x, shift, scale, g, wf, wl, wm, wg, bl)


def _scan_lanes(x, reverse):
    n = x.shape[-1]
    lane = lax.broadcasted_iota(jnp.int32, x.shape, x.ndim - 1)
    sh = 1
    while sh < n:
        if reverse:
            x = x + jnp.where(lane < n - sh, pltpu.roll(x, n - sh, axis=x.ndim - 1), 0.0)
        else:
            x = x + jnp.where(lane >= sh, pltpu.roll(x, sh, axis=x.ndim - 1), 0.0)
        sh *= 2
    return x


def _cumsum_kernel(x_ref, o_ref):
    o_ref[...] = _scan_lanes(x_ref[...], reverse=False)


def _cumsum(x):
    b, nh, s = x.shape
    return pl.pallas_call(
        _cumsum_kernel,
        grid=(b,),
        in_specs=[pl.BlockSpec((None, nh, s), lambda bi: (bi, 0, 0))],
        out_specs=pl.BlockSpec((None, nh, s), lambda bi: (bi, 0, 0)),
        out_shape=jax.ShapeDtypeStruct((b, nh, s), F32),
        compiler_params=_params("arbitrary"),
        name="logf_cumsum",
    )(x)


def _fox_aug_kernel(c_ref, aq_ref, ak_ref):
    c = c_ref[...] * LOG2E
    nh, tm = c.shape
    pieces = _split3(c)
    sub = lax.broadcasted_iota(jnp.int32, (LANES, tm), 0)
    row = lambda n, h: jnp.broadcast_to(pieces[n][h:h + 1, :], (LANES, tm))
    for h in range(nh):
        o = SUBLANES * (h % 2)
        mq = jnp.where((sub >= o + 3) & (sub < o + 6), 1.0, 0.0)
        for n in range(3):
            mq = jnp.where(sub == o + n, row(n, h), mq)
        aq_ref[:, h * LANES:(h + 1) * LANES] = mq.T.astype(BF16)
    for hp in range(nh // 2):
        mk = jnp.zeros((LANES, tm), F32)
        for hh in range(2):
            o = SUBLANES * hh
            mk = jnp.where((sub >= o) & (sub < o + 3), 1.0, mk)
            for n in range(3):
                mk = jnp.where(sub == o + 3 + n, -row(n, 2 * hp + hh), mk)
        ak_ref[:, hp * LANES:(hp + 1) * LANES] = mk.T.astype(BF16)


def _fox_aug(cum, *, tm):
    b, nh, s = cum.shape
    return pl.pallas_call(
        _fox_aug_kernel,
        grid=(b, s // tm),
        in_specs=[pl.BlockSpec((None, nh, tm), lambda bi, i: (bi, 0, i))],
        out_specs=[pl.BlockSpec((None, tm, nh * LANES), lambda bi, i: (bi, i, 0)),
                   pl.BlockSpec((None, tm, nh // 2 * LANES), lambda bi, i: (bi, i, 0))],
        out_shape=[jax.ShapeDtypeStruct((b, s, nh * LANES), BF16),
                   jax.ShapeDtypeStruct((b, s, nh // 2 * LANES), BF16)],
        compiler_params=_params("arbitrary", "arbitrary"),
        name="fox_aug",
    )(cum)


def _kmean_kernel(k_ref, o_ref, *, nvalid):
    i = pl.program_id(1)

    @pl.when(i < nvalid)
    def _():
        k = k_ref[...]
        o_ref[...] = jnp.mean(k.reshape(SUBLANES, MOBA_BLOCK, k.shape[-1]), axis=1)

    @pl.when(i >= nvalid)
    def _():
        o_ref[...] = jnp.zeros_like(o_ref)


def _kmean(k, nbp):
    b, s, w = k.shape
    rows = SUBLANES * MOBA_BLOCK
    nvalid = s // rows
    return pl.pallas_call(
        functools.partial(_kmean_kernel, nvalid=nvalid),
        grid=(b, nbp // SUBLANES),
        in_specs=[pl.BlockSpec((None, rows, w), lambda bi, i: (bi, jnp.minimum(i, nvalid - 1), 0))],
        out_specs=pl.BlockSpec((None, SUBLANES, w), lambda bi, i: (bi, i, 0)),
        out_shape=jax.ShapeDtypeStruct((b, nbp, w), F32),
        compiler_params=_params("arbitrary", "arbitrary"),
        name="moba_kmean",
    )(k)


def _pick_topk(g, idx, nbp):
    sel = jnp.zeros(g.shape, jnp.bool_)
    for _ in range(MOBA_TOPK):
        mx = jnp.max(g, axis=0, keepdims=True)
        is_max = (g == mx) & (mx > -jnp.inf)
        first = jnp.min(jnp.where(is_max, idx, nbp), axis=0, keepdims=True)
        pick = idx == first
        sel = sel | pick
        g = jnp.where(pick, -jnp.inf, g)
    return sel


def _moba_aug_kernel(slope_ref, q_ref, km_ref, aq_ref):
    t = MOBA_BLOCK
    hp = pl.program_id(1)
    i = pl.program_id(2)
    q2 = q_ref[...]
    km = km_ref[...]
    km_hi = km.astype(BF16)
    km_lo = (km - km_hi.astype(F32)).astype(BF16)
    lane = lax.broadcasted_iota(jnp.int32, (1, LANES), 1)
    blk = lax.broadcasted_iota(jnp.int32, (t, LANES), 1)
    blk_t = lax.broadcasted_iota(jnp.int32, (LANES, t), 0)
    qpos = (i * t + lax.broadcasted_iota(jnp.int32, (t, LANES), 0)).astype(F32)
    zero = jnp.zeros_like(q2)
    for hh in range(2):
        sig = jnp.full((t, LANES), slope_ref[2 * hp + hh] * LOG2E, F32)
        qh = jnp.where((lane < HEAD_DIM) if hh == 0 else (lane >= HEAD_DIM), q2, zero)
        gate = _dot_nt(km_hi, qh) + _dot_nt(km_lo, qh)
        sel = _pick_topk(jnp.where(blk_t < i, gate, -jnp.inf), blk_t, LANES)
        aug = jnp.where(sel | (blk_t == i), 0.0, -BIG).T
        s3 = _split3(sig)
        t3 = _split3(-sig * qpos)
        for n in range(3):
            aug = jnp.where((blk == A_BLK + n) | (blk == A_OFF + n), s3[n], aug)
            aug = jnp.where(blk == A_QPOS + n, t3[n], aug)
        aug = jnp.where(blk >= A_END, 0.0, aug)
        aq_ref[:, hh * LANES:(hh + 1) * LANES] = aug.astype(BF16)


def _moba_aug(slopes, q, kmean):
    b, s, w = q.shape
    npair = w // LANES
    t = MOBA_BLOCK
    return pl.pallas_call(
        _moba_aug_kernel,
        grid=(b, npair, s // t),
        in_specs=[pl.BlockSpec(memory_space=pltpu.SMEM),
                  pl.BlockSpec((None, t, LANES), lambda bi, hp, i: (bi, i, hp)),
                  pl.BlockSpec((None, LANES, LANES), lambda bi, hp, i: (bi, 0, hp))],
        out_specs=pl.BlockSpec((None, t, 2 * LANES), lambda bi, hp, i: (bi, i, hp)),
        out_shape=jax.ShapeDtypeStruct((b, s, 2 * w), BF16),
        compiler_params=_params("arbitrary", "arbitrary", "arbitrary"),
        name="moba_aug",
    )(slopes, q, kmean)


def _moba_key_table(s):
    pos = jnp.arange(s, dtype=jnp.int32)[:, None]
    lane = jnp.arange(LANES, dtype=jnp.int32)[None, :]
    blk = pos // MOBA_BLOCK
    tab = jnp.where(lane == blk, 1, 0)
    tab = jnp.where((lane >= A_BLK) & (lane < A_OFF), MOBA_BLOCK * blk, tab)
    tab = jnp.where((lane >= A_OFF) & (lane < A_QPOS), pos % MOBA_BLOCK, tab)
    tab = jnp.where((lane >= A_QPOS) & (lane < A_END), 1, tab)
    return tab.astype(BF16)[None]


def _attn_kernel(q_ref, aq_ref, k_ref, ak_ref, v_ref, o_ref, *, t, block_causal):
    i = pl.program_id(2)
    q2 = q_ref[...]
    lane = lax.broadcasted_iota(jnp.int32, (1, LANES), 1)
    zero = jnp.zeros_like(q2)
    qa = (jnp.concatenate([jnp.where(lane < HEAD_DIM, q2, zero), aq_ref[:, :LANES]], axis=1),
          jnp.concatenate([jnp.where(lane >= HEAD_DIM, q2, zero), aq_ref[:, LANES:]], axis=1))
    ones = jnp.ones((t, LANES), BF16)

    def step(s, va, m, acc):
        part = s[:, :LANES]
        for c in range(1, s.shape[1] // LANES):
            part = jnp.maximum(part, s[:, c * LANES:(c + 1) * LANES])
        m_new = jnp.maximum(m, jnp.max(part, axis=-1, keepdims=True))
        p = jnp.exp2(s - m_new)
        acc = jnp.exp2(m - m_new) * acc + _dot(p.astype(BF16), va)
        return m_new, acc

    def tiles(j):
        ks = pl.multiple_of(j * t, t)
        ka = jnp.concatenate([k_ref[pl.ds(ks, t), :], ak_ref[pl.ds(ks, t), :]], axis=1)
        v = v_ref[pl.ds(ks, t), :]
        return ka, (jnp.where(lane < HEAD_DIM, v, ones), jnp.where(lane >= HEAD_DIM, v, ones))

    def body(j, carry):
        ka, va = tiles(j)
        return tuple(step(_dot_nt(qa[hh], ka), va[hh], *carry[hh]) for hh in range(2))

    init = (jnp.full((t, 1), -jnp.inf, F32), jnp.zeros((t, LANES), F32))
    carry = lax.fori_loop(0, i, body, (init, init))

    ka, va = tiles(i)
    row = lax.broadcasted_iota(jnp.int32, (t, t), 0)
    col = lax.broadcasted_iota(jnp.int32, (t, t), 1)
    future = col > row
    if block_causal:
        future = future & ((col // MOBA_BLOCK) == (row // MOBA_BLOCK))
    outs = []
    for hh in range(2):
        s = jnp.where(future, -jnp.inf, _dot_nt(qa[hh], ka))
        _, acc = step(s, va[hh], *carry[hh])
        outs.append(acc / pltpu.roll(acc, HEAD_DIM, axis=1))
    o_ref[...] = jnp.where(lane < HEAD_DIM, outs[0], outs[1]).astype(BF16)


def _attention(q, aq, k, ak, v, *, t, block_causal, name):
    b, s, w = q.shape
    npair = w // LANES
    per_seq = ak.shape[0] > 1
    per_pair = ak.shape[2] > LANES
    return pl.pallas_call(
        functools.partial(_attn_kernel, t=t, block_causal=block_causal),
        grid=(b, npair, s // t),
        in_specs=[pl.BlockSpec((None, t, LANES), lambda bi, hp, i: (bi, i, hp)),
                  pl.BlockSpec((None, t, 2 * LANES), lambda bi, hp, i: (bi, i, hp)),
                  pl.BlockSpec((None, s, LANES), lambda bi, hp, i: (bi, 0, hp)),
                  pl.BlockSpec((None, s, LANES),
                               lambda bi, hp, i: (bi if per_seq else 0, 0, hp if per_pair else 0)),
                  pl.BlockSpec((None, s, LANES), lambda bi, hp, i: (bi, 0, hp))],
        out_specs=pl.BlockSpec((None, t, LANES), lambda bi, hp, i: (bi, i, hp)),
        out_shape=jax.ShapeDtypeStruct((b, s, w), BF16),
        compiler_params=_params("arbitrary", "arbitrary", "arbitrary"),
        name=name,
    )(q, aq, k, ak, v)


def _outproj_kernel(x_ref, of_ref, om_ref, sgf_ref, sgm_ref, gate_ref, g_ref, wbf_ref, wbm_ref, wo_ref, y_ref):
    yf = _dot(of_ref[...], wbf_ref[...])
    ym = _dot(om_ref[...], wbm_ref[...])
    mixed = sgf_ref[...].astype(F32) * yf + sgm_ref[...].astype(F32) * ym
    z = _dot(mixed.astype(BF16), wo_ref[...])
    y_ref[...] = x_ref[...] + gate_ref[...] * _rms(z, g_ref[...])


def _outproj(x, of, om, sgf, sgm, gate, g, wbf, wbm, wo, *, tm):
    b, s, d = x.shape
    if gate.shape[1] == 1:
        mod_spec = pl.BlockSpec((None, 1, d), lambda bi, i: (bi, 0, 0))
    else:
        mod_spec = pl.BlockSpec((None, tm, d), lambda bi, i: (bi, i, 0))
    row = lambda w: pl.BlockSpec((None, tm, w), lambda bi, i: (bi, i, 0))
    return pl.pallas_call(
        _outproj_kernel,
        grid=(b, s // tm),
        in_specs=[row(d), row(of.shape[-1]), row(om.shape[-1]), row(d), row(d), mod_spec,
                  _const_spec((1, d)), _const_spec(wbf.shape), _const_spec(wbm.shape), _const_spec(wo.shape)],
        out_specs=row(d),
        out_shape=jax.ShapeDtypeStruct((b, s, d), F32),
        compiler_params=_params("arbitrary", "arbitrary"),
        name="out_proj",
    )(x, of, om, sgf, sgm, gate, g, wbf, wbm, wo)


def _gelu_tanh(x):
    c = 0.7978845608028654
    return 0.5 * x * (1.0 + jnp.tanh(c * (x + 0.044715 * (x * x * x))))


def _ffn_kernel(*refs, dff, ch, tm, stepwise):
    if stepwise:
        (x_ref, shift_ref, scale_ref, gate_ref, gpre_ref, gpost_ref, wup_ref, wc_ref, bc_ref, wdn_ref,
         p2_ref, p1_ref, y_ref, u_ref, acc_ref) = refs
    else:
        (x_ref, shift_ref, scale_ref, gate_ref, gpre_ref, gpost_ref, wup_ref, wc_ref, bc_ref, wdn_ref,
         y_ref, tail_ref, acc_ref, ubuf_ref, carry_ref) = refs
        first = pl.program_id(1) == 0
    x = x_ref[...]
    h = (_rms(x, gpre_ref[...]) * (1.0 + scale_ref[...]) + shift_ref[...]).astype(BF16)
    acc_ref[...] = jnp.zeros_like(acc_ref)
    for c in range(dff // ch):
        cs = slice(c * ch, (c + 1) * ch)
        u = _dot(h, wup_ref[:, cs])
        v = _dot(h, wup_ref[:, dff + c * ch:dff + (c + 1) * ch])
        if stepwise:
            u2 = p2_ref[:, cs]
            u1 = p1_ref[:, cs]
            u_ref[:, cs] = u
        else:
            @pl.when(first)
            def _():
                carry_ref[c] = jnp.zeros((SUBLANES, ch), F32)

            ubuf_ref[0:SUBLANES, :] = carry_ref[c]
            ubuf_ref[SUBLANES:SUBLANES + tm, :] = u
            u1 = ubuf_ref[SUBLANES - 1:SUBLANES - 1 + tm, :]
            u2 = ubuf_ref[SUBLANES - 2:SUBLANES - 2 + tm, :]
            tail = u[tm - SUBLANES:, :]
            carry_ref[c] = tail
            tail_ref[:, cs] = tail
        uc = bc_ref[:, cs] + (wc_ref[0:1, cs] * u2 + wc_ref[1:2, cs] * u1 + wc_ref[2:3, cs] * u)
        g = (_gelu_tanh(uc) * v).astype(BF16)
        acc_ref[...] += _dot(g, wdn_ref[cs, :])
    y_ref[...] = x + gate_ref[...] * _rms(acc_ref[...], gpost_ref[...])


def _ffn(x, shift, scale, gate, gpre, gpost, wup, wc, bc, wdn, prev=None, *, tm):
    b, s, d = x.shape
    dff = wdn.shape[0]
    ch = 256 if dff % 256 == 0 else LANES
    stepwise = prev is not None
    if shift.shape[1] == 1:
        mod_spec = pl.BlockSpec((None, 1, d), lambda bi, i: (bi, 0, 0))
    else:
        mod_spec = pl.BlockSpec((None, tm, d), lambda bi, i: (bi, i, 0))
    row = lambda w: pl.BlockSpec((None, tm, w), lambda bi, i: (bi, i, 0))
    in_specs = [row(d), mod_spec, mod_spec, mod_spec, _const_spec((1, d)), _const_spec((1, d)),
                _const_spec(wup.shape), _const_spec(wc.shape), _const_spec(bc.shape), _const_spec(wdn.shape)]
    args = [x, shift, scale, gate, gpre, gpost, wup, wc, bc, wdn]
    scratch = [pltpu.VMEM((tm, d), F32)]
    if stepwise:
        in_specs += [row(dff), row(dff)]
        args += list(prev)
        out_specs = [row(d), row(dff)]
        out_shape = [jax.ShapeDtypeStruct((b, s, d), F32), jax.ShapeDtypeStruct((b, s, dff), F32)]
    else:
        out_specs = [row(d), pl.BlockSpec((None, SUBLANES, dff), lambda bi, i: (bi, 0, 0))]
        out_shape = [jax.ShapeDtypeStruct((b, s, d), F32), jax.ShapeDtypeStruct((b, SUBLANES, dff), F32)]
        scratch += [pltpu.VMEM((tm + SUBLANES, ch), F32), pltpu.VMEM((dff // ch, SUBLANES, ch), F32)]
    return pl.pallas_call(
        functools.partial(_ffn_kernel, dff=dff, ch=ch, tm=tm, stepwise=stepwise),
        grid=(b, s // tm),
        in_specs=in_specs,
        out_specs=out_specs,
        out_shape=out_shape,
        scratch_shapes=scratch,
        compiler_params=_params("arbitrary", "arbitrary"),
        name="ffn_step" if stepwise else "ffn_seq",
    )(*args)


def _page_spec(block, layer, npages_per_seq, per_step, p, seq_of=lambda bi: bi):
    def index(bi, g, pt_ref):
        page = pt_ref[seq_of(bi) * npages_per_seq + g * per_step + p]
        return (layer, page) + (0,) * (len(block) - 2)
    return pl.BlockSpec(block, index)


def _col_spec(nh, hd, seq_of=lambda bi: bi):
    return pl.BlockSpec((None, nh, hd, 1), lambda bi, g, pt: (seq_of(bi), 0, 0, 0))


ROW_BLK = 2 * SUBLANES


def _rows_spec(w, seq_of=lambda bi: bi):
    return pl.BlockSpec((None, ROW_BLK, w), lambda bi, g, pt: (0, seq_of(bi) // ROW_BLK, 0))


def _head_rows(ref, seq, nh):
    row = ref[pl.ds(seq % ROW_BLK, 1), :]
    hd = row.shape[1] // nh
    return [row[:, h * hd:(h + 1) * hd] for h in range(nh)]


def _row_to_col(row):
    n = row.shape[1]
    eye = lax.broadcasted_iota(jnp.int32, (n, n), 0) == lax.broadcasted_iota(jnp.int32, (n, n), 1)
    return jnp.sum(jnp.where(eye, jnp.broadcast_to(row, (n, n)), 0.0), axis=1, keepdims=True)


def _stack_rows(rows):
    shape = (len(rows), rows[0].shape[1])
    sub = lax.broadcasted_iota(jnp.int32, shape, 0)
    out = jnp.broadcast_to(rows[0], shape)
    for h in range(1, len(rows)):
        out = jnp.where(sub == h, jnp.broadcast_to(rows[h], shape), out)
    return out


def _page_scores(k_ref, qb_ref):
    nh = k_ref.shape[0]
    return _stack_rows([jnp.sum(k_ref[h] * qb_ref[h], axis=0, keepdims=True) for h in range(nh)])


def _logf_bias_kernel(pt_ref, new_ref, *refs, per_step):
    pages = refs[:per_step]
    o_ref, x_ref = refs[per_step:]
    g = pl.program_id(1)
    rows = pages[0].shape[-1]
    for p in range(per_step):
        x_ref[:, pl.ds(pl.multiple_of((g * per_step + p) * rows, rows), rows)] = pages[p][...]

    @pl.when(g == pl.num_programs(1) - 1)
    def _():
        x = x_ref[...]
        n = x.shape[-1]
        lane = lax.broadcasted_iota(jnp.int32, x.shape, 1)
        shifted = jnp.where(lane < n - 1, pltpu.roll(x, n - 1, axis=1), 0.0)
        o_ref[...] = (_scan_lanes(shifted, reverse=True) + new_ref[...]) * LOG2E


def _logf_bias(cache_t, layer, new, pt_flat, npages, per_step):
    _, _, nh, rows = cache_t.shape
    b = new.shape[0]
    n = npages * rows
    grid_spec = pltpu.PrefetchScalarGridSpec(
        num_scalar_prefetch=1,
        grid=(b, npages // per_step),
        in_specs=[pl.BlockSpec((None, nh, 1), lambda bi, g, pt: (bi, 0, 0))]
                 + [_page_spec((None, None, nh, rows), layer, npages, per_step, p) for p in range(per_step)],
        out_specs=pl.BlockSpec((None, nh, n), lambda bi, g, pt: (bi, 0, 0)),
        scratch_shapes=[pltpu.VMEM((nh, n), F32)],
    )
    return pl.pallas_call(
        functools.partial(_logf_bias_kernel, per_step=per_step),
        grid_spec=grid_spec,
        out_shape=jax.ShapeDtypeStruct((b, nh, n), F32),
        compiler_params=_params("arbitrary", "arbitrary"),
        name="logf_bias",
    )(pt_flat, new, *([cache_t] * per_step))


def _fox_dec_kernel(pt_ref, q_ref, kn_ref, vn_ref, d_ref, *refs, per_step):
    kp = refs[:per_step]
    vp = refs[per_step:2 * per_step]
    o_ref, qb_ref, m_ref, l_ref, acc_ref = refs[2 * per_step:]
    seq = pl.program_id(0)
    g = pl.program_id(1)
    nh, hd, rows = kp[0].shape

    @pl.when(g == 0)
    def _():
        q = _head_rows(q_ref, seq, nh)
        for h in range(nh):
            qb_ref[h] = jnp.broadcast_to(_row_to_col(q[h]), (hd, rows))
        m_ref[...] = jnp.full_like(m_ref, -BIG)
        l_ref[...] = jnp.zeros_like(l_ref)
        acc_ref[...] = jnp.zeros_like(acc_ref)

    ss = []
    m_new = m_ref[...]
    for p in range(per_step):
        s = _page_scores(kp[p], qb_ref) + d_ref[:, p * rows:(p + 1) * rows]
        ss.append(s)
        m_new = jnp.maximum(m_new, jnp.max(s, axis=-1, keepdims=True))
    alpha = jnp.exp2(m_ref[...] - m_new)
    l = alpha * l_ref[...]
    prs = []
    for p in range(per_step):
        pr = jnp.exp2(ss[p] - m_new)
        prs.append(pr)
        l = l + jnp.sum(pr, axis=-1, keepdims=True)
    for h in range(nh):
        acc = alpha[h:h + 1, :] * acc_ref[h]
        for p in range(per_step):
            acc = acc + vp[p][h] * prs[p][h:h + 1, :]
        acc_ref[h] = acc
    m_ref[...] = m_new
    l_ref[...] = l

    @pl.when(g == pl.num_programs(1) - 1)
    def _():
        q, kn, vn = _head_rows(q_ref, seq, nh), _head_rows(kn_ref, seq, nh), _head_rows(vn_ref, seq, nh)
        for h in range(nh):
            s_self = jnp.sum(q[h] * kn[h], axis=1, keepdims=True)
            m_h = m_new[h:h + 1, :]
            m_fin = jnp.maximum(m_h, s_self)
            a = jnp.exp2(m_h - m_fin)
            p_self = jnp.exp2(s_self - m_fin)
            num = a * jnp.sum(acc_ref[h], axis=-1, keepdims=True) + p_self * _row_to_col(vn[h])
            o_ref[h] = num / (a * l[h:h + 1, :] + p_self)


def _fox_decode(q, kn, vn, bias, cache_k, cache_v, layer, pt_flat, npages, per_step):
    b, w = q.shape[1], q.shape[2]
    _, _, nh, hd, rows = cache_k.shape
    page = lambda p: _page_spec((None, None, nh, hd, rows), layer, npages, per_step, p)
    grid_spec = pltpu.PrefetchScalarGridSpec(
        num_scalar_prefetch=1,
        grid=(b, npages // per_step),
        in_specs=[_rows_spec(w), _rows_spec(w), _rows_spec(w),
                  pl.BlockSpec((None, nh, per_step * rows), lambda bi, g, pt: (bi, 0, g))]
                 + [page(p) for p in range(per_step)] + [page(p) for p in range(per_step)],
        out_specs=_col_spec(nh, hd),
        scratch_shapes=[pltpu.VMEM((nh, hd, rows), F32), pltpu.VMEM((nh, 1), F32), pltpu.VMEM((nh, 1), F32),
                        pltpu.VMEM((nh, hd, rows), F32)],
    )
    return pl.pallas_call(
        functools.partial(_fox_dec_kernel, per_step=per_step),
        grid_spec=grid_spec,
        out_shape=jax.ShapeDtypeStruct((b, nh, hd, 1), F32),
        compiler_params=_params("arbitrary", "arbitrary"),
        name="fox_decode",
    )(pt_flat, q, kn, vn, bias, *([cache_k] * per_step), *([cache_v] * per_step))


def _moba_dec_kernel(pt_ref, slope_ref, q_ref, kn_ref, vn_ref, *refs, per_step, pages_per_block, nblocks):
    kp = refs[:per_step]
    vp = refs[per_step:2 * per_step]
    o_ref, qb_ref, s_ref, gate_ref, w_ref, l_ref, acc_ref = refs[2 * per_step:]
    bb = pl.program_id(0)
    g = pl.program_id(1)
    nseq = pl.num_programs(0) - 1
    ng = pl.num_programs(1)
    nh, hd, rows = kp[0].shape
    npages = nblocks * pages_per_block
    past = npages * rows
    lane = lax.broadcasted_iota(jnp.int32, (nh, LANES), 1)
    slot = bb % 2
    prev = 1 - slot
    keys_on = bb < nseq
    vals_on = bb >= 1
    kseq = jnp.minimum(bb, nseq - 1)
    vseq = jnp.maximum(bb - 1, 0)

    @pl.when(keys_on & (g == 0))
    def _():
        q = _head_rows(q_ref, kseq, nh)
        for h in range(nh):
            qb_ref[h] = jnp.broadcast_to(_row_to_col(q[h]), (hd, rows))
        gate_ref[...] = jnp.zeros_like(gate_ref)

    @pl.when(keys_on)
    def _():
        gate = gate_ref[...]
        for p in range(per_step):
            page = g * per_step + p
            s = _page_scores(kp[p], qb_ref)
            s_ref[slot, :, pl.ds(pl.multiple_of(page * rows, rows), rows)] = s
            gate = gate + jnp.where(lane == page // pages_per_block, jnp.sum(s, axis=-1, keepdims=True), 0.0)
        gate_ref[...] = gate

    @pl.when(keys_on & (g == ng - 1))
    def _():
        gsel = jnp.where(lane < nblocks, gate_ref[...], -jnp.inf)
        chosen = jnp.zeros((nh, LANES), F32)
        for _ in range(MOBA_TOPK):
            mx = jnp.max(gsel, axis=-1, keepdims=True)
            first = jnp.min(jnp.where(gsel == mx, lane, LANES), axis=-1, keepdims=True)
            chosen = jnp.where(lane == first, 1.0, chosen)
            gsel = jnp.where(lane == first, -jnp.inf, gsel)
        sub = lax.broadcasted_iota(jnp.int32, (nh, 1), 0)
        sig = jnp.zeros((nh, 1), F32)
        for h in range(nh):
            sig = jnp.where(sub == h, slope_ref[h] * LOG2E, sig)
        picked = jnp.concatenate(
            [jnp.broadcast_to(jnp.sum(jnp.where(lane == n, chosen, 0.0), axis=-1, keepdims=True),
                              (nh, MOBA_BLOCK)) for n in range(nblocks)], axis=1)
        pos = lax.broadcasted_iota(jnp.int32, (nh, past), 1)
        sb = jnp.where(picked > 0.0, s_ref[slot] - sig * (past - pos).astype(F32), -BIG)
        q, kn = _head_rows(q_ref, kseq, nh), _head_rows(kn_ref, kseq, nh)
        s_self = _stack_rows([jnp.sum(q[h] * kn[h], axis=1, keepdims=True) for h in range(nh)])
        m = jnp.maximum(jnp.max(sb, axis=-1, keepdims=True), s_self)
        pr = jnp.exp2(sb - m)
        w_self = jnp.exp2(s_self - m)
        s_ref[slot] = pr
        w_ref[slot] = w_self
        l_ref[slot] = jnp.sum(pr, axis=-1, keepdims=True) + w_self

    @pl.when(vals_on & (g == 0))
    def _():
        acc_ref[...] = jnp.zeros_like(acc_ref)

    @pl.when(vals_on)
    def _():
        for h in range(nh):
            acc = acc_ref[h]
            for p in range(per_step):
                page = g * per_step + p
                pr = s_ref[prev, pl.ds(h, 1), pl.ds(pl.multiple_of(page * rows, rows), rows)]
                acc = acc + vp[p][h] * pr
            acc_ref[h] = acc

    @pl.when(vals_on & (g == ng - 1))
    def _():
        w_self = w_ref[prev]
        l = l_ref[prev]
        vn = _head_rows(vn_ref, vseq, nh)
        for h in range(nh):
            num = jnp.sum(acc_ref[h], axis=-1, keepdims=True) + w_self[h:h + 1, :] * _row_to_col(vn[h])
            o_ref[h] = num / l[h:h + 1, :]


def _moba_decode(q, kn, vn, slopes, cache_k, cache_v, layer, pt_flat, npages, per_step):
    b, w = q.shape[1], q.shape[2]
    _, _, nh, hd, rows = cache_k.shape
    ng = npages // per_step
    pages_per_block = MOBA_BLOCK // rows
    kseq = lambda bi: jnp.minimum(bi, b - 1)
    vseq = lambda bi: jnp.maximum(bi - 1, 0)
    kpage = lambda p: _page_spec((None, None, nh, hd, rows), layer, npages, per_step, p, seq_of=kseq)
    vpage = lambda p: _page_spec((None, None, nh, hd, rows), layer, npages, per_step, p, seq_of=vseq)
    grid_spec = pltpu.PrefetchScalarGridSpec(
        num_scalar_prefetch=1,
        grid=(b + 1, ng),
        in_specs=[pl.BlockSpec(memory_space=pltpu.SMEM), _rows_spec(w, kseq), _rows_spec(w, kseq),
                  _rows_spec(w, vseq)]
                 + [kpage(p) for p in range(per_step)] + [vpage(p) for p in range(per_step)],
        out_specs=_col_spec(nh, hd, vseq),
        scratch_shapes=[pltpu.VMEM((nh, hd, rows), F32), pltpu.VMEM((2, nh, npages * rows), F32),
                        pltpu.VMEM((nh, LANES), F32), pltpu.VMEM((2, nh, 1), F32), pltpu.VMEM((2, nh, 1), F32),
                        pltpu.VMEM((nh, hd, rows), F32)],
    )
    return pl.pallas_call(
        functools.partial(_moba_dec_kernel, per_step=per_step, pages_per_block=pages_per_block,
                          nblocks=npages // pages_per_block),
        grid_spec=grid_spec,
        out_shape=jax.ShapeDtypeStruct((b, nh, hd, 1), F32),
        compiler_params=_params("arbitrary", "arbitrary"),
        name="moba_decode",
    )(pt_flat, slopes, q, kn, vn, *([cache_k] * per_step), *([cache_v] * per_step))


def _split_w_in(w_in, nh_fox, fw, mw, d):
    o1 = 3 * fw
    o2 = o1 + nh_fox
    o3 = o2 + 3 * mw
    wf = w_in[:, :o1].astype(BF16)
    wl = jnp.pad(w_in[:, o1:o2], ((0, 0), (0, LANES - nh_fox))).astype(BF16)
    wm = w_in[:, o2:o3].astype(BF16)
    wg = w_in[:, o3:].astype(BF16)
    return wf, wl, wm, wg


def kernel(x_prompt, x_sample, cache_fox_k, cache_fox_v, cache_fox_logf, cache_moba_k, cache_moba_v,
           state_conv, page_table, c_prompt, c_sample, w_ada, b_ada, g_mix_pre, g_mix_post, w_in, b_forget,
           w_branch_fox, w_branch_moba, w_out, g_ffn_pre, g_ffn_post, w_up, w_conv, b_conv, w_down):
    depth = w_ada.shape[0]
    bp, s, d = x_prompt.shape
    bs, dec_s, _ = x_sample.shape
    nh_fox = b_forget.shape[1]
    fw = w_branch_fox.shape[1]
    mw = w_branch_moba.shape[1]
    nh_moba = mw // HEAD_DIM
    page_rows = cache_fox_k.shape[2]
    npages = page_table.shape[1]
    past = npages * page_rows
    tm = 512
    t_attn = 1024
    per_step = min(32, npages)
    assert dec_s == 1 and fw == nh_fox * HEAD_DIM and nh_fox == SUBLANES and nh_moba == SUBLANES
    assert fw % LANES == 0 and mw % LANES == 0 and s % (SUBLANES * MOBA_BLOCK) == 0 and s % tm == 0
    assert s // MOBA_BLOCK <= SEL_LANES and s % t_attn == 0 and t_attn % MOBA_BLOCK == 0
    assert page_rows == LANES and MOBA_BLOCK % page_rows == 0 and npages % per_step == 0 and bs % ROW_BLK == 0
    assert past % MOBA_BLOCK == 0 and MOBA_TOPK <= past // MOBA_BLOCK <= LANES
    slopes = jnp.asarray([2.0 ** (-8.0 * (h + 1) / nh_moba) for h in range(nh_moba)], F32)
    pt_flat = page_table.reshape(-1).astype(jnp.int32)
    moba_keys = _moba_key_table(s)
    fk_t, fv_t = jnp.transpose(cache_fox_k, (0, 1, 3, 4, 2)), jnp.transpose(cache_fox_v, (0, 1, 3, 4, 2))
    mk_t, mv_t = jnp.transpose(cache_moba_k, (0, 1, 3, 4, 2)), jnp.transpose(cache_moba_v, (0, 1, 3, 4, 2))
    fl_t = jnp.transpose(cache_fox_logf, (0, 1, 3, 2))

    rows_c = bp + bs
    rows_pad = -(-rows_c // SUBLANES) * SUBLANES
    c_all = jnp.pad(jnp.concatenate([c_prompt, c_sample], axis=0), ((0, rows_pad - rows_c), (0, 0)))

    y_p, y_s = x_prompt, x_sample.reshape(1, bs, d)
    outs_p, outs_s = [], []
    for l in range(depth):
        mod = _ada(c_all, w_ada[l], b_ada[l][None, :]).reshape(rows_pad, N_MOD, d)
        mod_p = [mod[:bp, i][:, None, :] for i in range(N_MOD)]
        mod_s = [mod[bp:rows_c, i][None] for i in range(N_MOD)]
        wf, wl, wm, wg = _split_w_in(w_in[l], nh_fox, fw, mw, d)
        bl = jnp.pad(b_forget[l], (0, LANES - nh_fox))[None, :]
        g1, g2 = g_mix_pre[l][None, :], g_mix_post[l][None, :]
        g3, g4 = g_ffn_pre[l][None, :], g_ffn_post[l][None, :]
        wbf, wbm, wo = w_branch_fox[l].astype(BF16), w_branch_moba[l].astype(BF16), w_out[l].astype(BF16)
        wup, wdn = w_up[l].astype(BF16), w_down[l].astype(BF16)
        wc, bc = w_conv[l], b_conv[l][None, :]

        (qf, kf, vf, kfb, vfb, lf, lft, qm, km, vm, kmb, vmb, sgf, sgm) = _inproj(
            y_p, mod_p[0], mod_p[1], g1, wf, wl, wm, wg, bl, tm=tm, nh=nh_fox)
        aq_f, ak_f = _fox_aug(_cumsum(lft), tm=tm)
        o_f = _attention(qf, aq_f, kfb, ak_f, vfb, t=t_attn, block_causal=False, name="fox_attention")
        aq_m = _moba_aug(slopes, qm, _kmean(km, LANES))
        o_m = _attention(qm, aq_m, kmb, moba_keys, vmb, t=t_attn, block_causal=True, name="moba_attention")
        x1 = _outproj(y_p, o_f, o_m, sgf, sgm, mod_p[2], g2, wbf, wbm, wo, tm=tm)
        y_p, tail = _ffn(x1, mod_p[3], mod_p[4], mod_p[5], g3, g4, wup, wc, bc, wdn, tm=tm)
        heads = lambda t, n: t.reshape(bp, s, n, HEAD_DIM)
        outs_p.append((heads(kf, nh_fox), heads(vf, nh_fox), lf, heads(km, nh_moba), heads(vm, nh_moba),
                       tail[:, SUBLANES - (CONV_W - 1):]))

        (qf, kf, vf, _, _, lf, _, qm, km, vm, _, _, sgf, sgm) = _inproj(
            y_s, mod_s[0], mod_s[1], g1, wf, wl, wm, wg, bl, tm=bs, nh=nh_fox)
        bias = _logf_bias(fl_t, l, lf.reshape(bs, nh_fox, 1), pt_flat, npages, npages)
        o_f = _fox_decode(qf.astype(F32), kf, vf, bias, fk_t, fv_t, l, pt_flat, npages, per_step)
        o_m = _moba_decode(qm.astype(F32), km, vm, slopes, mk_t, mv_t, l, pt_flat, npages, per_step)
        o_f = o_f.reshape(1, bs, fw).astype(BF16)
        o_m = o_m.reshape(1, bs, mw).astype(BF16)
        x1 = _outproj(y_s, o_f, o_m, sgf, sgm, mod_s[2], g2, wbf, wbm, wo, tm=bs)
        prev = (state_conv[l][None, :, 0], state_conv[l][None, :, 1])
        y_s, u_new = _ffn(x1, mod_s[3], mod_s[4], mod_s[5], g3, g4, wup, wc, bc, wdn, prev, tm=bs)
        heads = lambda t, n: t.reshape(bs, 1, n, HEAD_DIM)
        outs_s.append((heads(kf, nh_fox), heads(vf, nh_fox), lf.reshape(bs, 1, nh_fox),
                       heads(km, nh_moba), heads(vm, nh_moba),
                       jnp.stack([state_conv[l][:, 1], u_new[0]], axis=1)))

    stack = lambda rows, i: jnp.stack([r[i] for r in rows])
    return (y_p, y_s.reshape(bs, 1, d),
            stack(outs_p, 0), stack(outs_p, 1), stack(outs_p, 2), stack(outs_p, 3), stack(outs_p, 4), stack(outs_p, 5),
            stack(outs_s, 0), stack(outs_s, 1), stack(outs_s, 2), stack(outs_s, 3), stack(outs_s, 4), stack(outs_s, 5))
```
